```python
import jax, jax.numpy as jnp
from jax import lax
import numpy as np

D_MODEL = 1024
BATCH = 4
SEQ = 4096
DEPTH = 4
DEC_BATCH = 128
DEC_SEQ = 8
PAST_LEN = 2048
PAGE_SIZE = 128

D_MIX = D_MODEL
D_CONV = D_MIX // 2
D_ATTN = D_MIX - D_CONV
HEAD_DIM = 64
N_HEADS = D_ATTN // HEAD_DIM
D_IN = 2 * D_CONV + 3 * D_ATTN
CONV_WIDTH = 31
MOBA_BLOCK = 256
MOBA_TOPK = 3
Q_CHUNK = 128
D_FF = 4 * D_MODEL
ALPHA = (2 * DEPTH) ** 0.25
BETA = (8 * DEPTH) ** -0.25
LN_EPS = 1e-5

kernel_name = "hymba_conformer_moba_deepnorm_step"


def layer_norm(x, g, b):
    xf = x.astype(jnp.float32)
    mu = jnp.mean(xf, -1, keepdims=True)
    var = jnp.mean(jnp.square(xf - mu), -1, keepdims=True)
    return ((xf - mu) * lax.rsqrt(var + LN_EPS) * g + b).astype(x.dtype)


def rms_norm(x, g):
    xf = x.astype(jnp.float32)
    return (xf * lax.rsqrt(jnp.mean(jnp.square(xf), -1, keepdims=True) + LN_EPS) * g).astype(x.dtype)


def conv_mixer(a, gt, conv_past, w_dw, b_dw, ln_g, ln_b):
    u = a * jax.nn.sigmoid(gt)
    u_ext = jnp.concatenate([conv_past, u], axis=1)
    y = lax.conv_general_dilated(u_ext, w_dw[:, None, :], (1,), 'VALID',
                                 dimension_numbers=('NWC', 'WIO', 'NWC'),
                                 feature_group_count=D_CONV) + b_dw
    y = jax.nn.silu(layer_norm(y, ln_g, ln_b))
    return y, u_ext[:, -(CONV_WIDTH - 1):]


def to_blocks(k, nb):
    b, t = k.shape[:2]
    k = jnp.pad(k, ((0, 0), (0, nb * MOBA_BLOCK - t), (0, 0), (0, 0)))
    return k.reshape(b, nb, MOBA_BLOCK, N_HEADS, HEAD_DIM).transpose(0, 3, 1, 2, 4)


def block_mean(kb):
    return jnp.mean(kb.astype(jnp.float32), axis=3).astype(kb.dtype)


def moba_query_block(q, q_pos, kb, vb, kmean):
    b, nq = q.shape[:2]
    nb = kb.shape[2]
    q_blk = q_pos // MOBA_BLOCK
    qt = jnp.swapaxes(q, 1, 2)
    gate = jnp.einsum('bhqd,bhnd->bhqn', qt, kmean, preferred_element_type=jnp.float32)
    past = jnp.arange(nb, dtype=jnp.int32)[None, :] < q_blk[:, None]
    gate = jnp.where(past, gate, -jnp.inf)
    _, top_i = lax.top_k(gate, min(MOBA_TOPK, nb))
    sel_ok = top_i < q_blk[:, None]
    own = jnp.broadcast_to(q_blk[:, None], top_i.shape[:-1] + (1,))
    idx = jnp.concatenate([top_i, own], axis=-1)
    ok = jnp.concatenate([sel_ok, jnp.ones(own.shape, dtype=bool)], axis=-1)
    bi = jnp.arange(b)[:, None, None, None]
    hi = jnp.arange(N_HEADS)[None, :, None, None]
    kg = kb[bi, hi, idx]
    vg = vb[bi, hi, idx]
    s = jnp.einsum('bhqd,bhqsnd->bhqsn', qt, kg, preferred_element_type=jnp.float32) * (HEAD_DIM ** -0.5)
    kpos = idx[..., None] * MOBA_BLOCK + jnp.arange(MOBA_BLOCK, dtype=jnp.int32)
    mask = ok[..., None] & (kpos <= q_pos[:, None, None])
    s = jnp.where(mask, s, -jnp.inf)
    p = jax.nn.softmax(s.reshape(b, N_HEADS, nq, -1), axis=-1).reshape(s.shape)
    return jnp.einsum('bhqsn,bhqsnd->bqhd', p.astype(vg.dtype), vg)


def moba_prompt(q, k, v):
    b, s = q.shape[:2]
    nb = -(-s // MOBA_BLOCK)
    kb, vb = to_blocks(k, nb), to_blocks(v, nb)
    kmean = block_mean(kb)
    n_chunks = s // Q_CHUNK
    qc = jnp.swapaxes(q.reshape(b, n_chunks, Q_CHUNK, N_HEADS, HEAD_DIM), 0, 1)
    starts = jnp.arange(n_chunks, dtype=jnp.int32) * Q_CHUNK

    def one(args):
        q_blk, s0 = args
        return moba_query_block(q_blk, s0 + jnp.arange(Q_CHUNK, dtype=jnp.int32), kb, vb, kmean)

    o = lax.map(one, (qc, starts))
    return jnp.swapaxes(o, 0, 1).reshape(b, s, D_ATTN)


def moba_sample(q, k_new, v_new, ck, cv, page_table):
    bd, l = q.shape[:2]
    past_len = page_table.shape[1] * PAGE_SIZE
    k_past = ck[page_table].reshape(bd, past_len, N_HEADS, HEAD_DIM)
    v_past = cv[page_table].reshape(bd, past_len, N_HEADS, HEAD_DIM)
    k_all = jnp.concatenate([k_past, k_new], axis=1)
    v_all = jnp.concatenate([v_past, v_new], axis=1)
    nb = -(-(past_len + l) // MOBA_BLOCK)
    kb, vb = to_blocks(k_all, nb), to_blocks(v_all, nb)
    kmean = block_mean(kb)
    o = moba_query_block(q, past_len + jnp.arange(l, dtype=jnp.int32), kb, vb, kmean)
    return o.reshape(bd, l, D_ATTN)


def trunk_layer(x, c, conv_past, attend, w_ada, b_ada, w_in, w_dw, b_dw, conv_ln_g, conv_ln_b,
                out_g_conv, out_g_attn, w_out, ln1_g, ln1_b, w1, b1, w2, b2, ln2_g, ln2_b):
    bsz, l = x.shape[:2]
    mod = jax.nn.silu(c) @ w_ada + b_ada
    sh1, sc1, g1, sh2, sc2, g2 = jnp.split(mod[:, None, :], 6, axis=-1)
    h = x * (1 + sc1) + sh1
    z = h @ w_in
    a, gt, q, k, v = jnp.split(z, [D_CONV, 2 * D_CONV, 2 * D_CONV + D_ATTN, 2 * D_CONV + 2 * D_ATTN], axis=-1)
    yc, conv_new = conv_mixer(a, gt, conv_past, w_dw, b_dw, conv_ln_g, conv_ln_b)
    q = q.reshape(bsz, l, N_HEADS, HEAD_DIM)
    k = k.reshape(bsz, l, N_HEADS, HEAD_DIM)
    v = v.reshape(bsz, l, N_HEADS, HEAD_DIM)
    ya = attend(q, k, v)
    mix = jnp.concatenate([rms_norm(yc, out_g_conv), rms_norm(ya, out_g_attn)], axis=-1) @ w_out
    x = layer_norm(ALPHA * x + (1 + g1) * mix, ln1_g, ln1_b)
    h = x * (1 + sc2) + sh2
    f = jnp.square(jax.nn.relu(h @ w1 + b1)) @ w2 + b2
    x = layer_norm(ALPHA * x + (1 + g2) * f, ln2_g, ln2_b)
    return x, conv_new, k, v


def setup_inputs(seed: int = 0) -> dict:
    key = jax.random.key(seed)
    ks = jax.random.split(key, 32)
    n_pages = PAST_LEN // PAGE_SIZE
    n_used = DEC_BATCH * n_pages
    n_pool = n_used + n_used // 4
    f32 = jnp.float32

    def nrm(k, shape, scale):
        return jax.random.normal(k, shape, f32) * scale

    page_table = jax.random.permutation(ks[0], n_pool)[:n_used].reshape(DEC_BATCH, n_pages).astype(jnp.int32)
    return {
        "x_prompt": nrm(ks[1], (BATCH, SEQ, D_MODEL), 1.0),
        "x_sample": nrm(ks[2], (DEC_BATCH, DEC_SEQ, D_MODEL), 1.0),
        "cache_k": nrm(ks[3], (DEPTH, n_pool, PAGE_SIZE, N_HEADS, HEAD_DIM), 1.0),
        "cache_v": nrm(ks[4], (DEPTH, n_pool, PAGE_SIZE, N_HEADS, HEAD_DIM), 1.0),
        "state_conv": nrm(ks[5], (DEPTH, DEC_BATCH, CONV_WIDTH - 1, D_CONV), 0.5),
        "page_table": page_table,
        "c_prompt": nrm(ks[6], (BATCH, D_MODEL), 1.0),
        "c_sample": nrm(ks[7], (DEC_BATCH, D_MODEL), 1.0),
        "ln0_g": 1.0 + nrm(ks[8], (D_MODEL,), 0.05),
        "ln0_b": nrm(ks[9], (D_MODEL,), 0.02),
        "w_ada": nrm(ks[10], (DEPTH, D_MODEL, 6 * D_MODEL), 0.1 * D_MODEL ** -0.5),
        "b_ada": nrm(ks[11], (DEPTH, 6 * D_MODEL), 0.01),
        "w_in": nrm(ks[12], (DEPTH, D_MODEL, D_IN), D_MODEL ** -0.5),
        "w_dw": nrm(ks[13], (DEPTH, CONV_WIDTH, D_CONV), CONV_WIDTH ** -0.5),
        "b_dw": nrm(ks[14], (DEPTH, D_CONV), 0.02),
        "conv_ln_g": 1.0 + nrm(ks[15], (DEPTH, D_CONV), 0.05),
        "conv_ln_b": nrm(ks[16], (DEPTH, D_CONV), 0.02),
        "out_g_conv": 1.0 + nrm(ks[17], (DEPTH, D_CONV), 0.05),
        "out_g_attn": 1.0 + nrm(ks[18], (DEPTH, D_ATTN), 0.05),
        "w_out": nrm(ks[19], (DEPTH, D_MIX, D_MODEL), BETA * D_MIX ** -0.5),
        "ln1_g": 1.0 + nrm(ks[20], (DEPTH, D_MODEL), 0.05),
        "ln1_b": nrm(ks[21], (DEPTH, D_MODEL), 0.02),
        "w1": nrm(ks[22], (DEPTH, D_MODEL, D_FF), D_MODEL ** -0.5),
        "b1": nrm(ks[23], (DEPTH, D_FF), 0.02),
        "w2": nrm(ks[24], (DEPTH, D_FF, D_MODEL), BETA * D_FF ** -0.5),
        "b2": nrm(ks[25], (DEPTH, D_MODEL), 0.02),
        "ln2_g": 1.0 + nrm(ks[26], (DEPTH, D_MODEL), 0.05),
        "ln2_b": nrm(ks[27], (DEPTH, D_MODEL), 0.02),
    }


def reference(x_prompt, x_sample, cache_k, cache_v, state_conv, page_table, c_prompt, c_sample,
              ln0_g, ln0_b, w_ada, b_ada, w_in, w_dw, b_dw, conv_ln_g, conv_ln_b, out_g_conv, out_g_attn,
              w_out, ln1_g, ln1_b, w1, b1, w2, b2, ln2_g, ln2_b):
    xp = layer_norm(x_prompt, ln0_g, ln0_b)
    xs = layer_norm(x_sample, ln0_g, ln0_b)
    conv_zero = jnp.zeros((x_prompt.shape[0], CONV_WIDTH - 1, D_CONV), x_prompt.dtype)
    kp, vp, cp, ksm, vsm, csm = [], [], [], [], [], []
    for l in range(DEPTH):
        wl = (w_ada[l], b_ada[l], w_in[l], w_dw[l], b_dw[l], conv_ln_g[l], conv_ln_b[l],
              out_g_conv[l], out_g_attn[l], w_out[l], ln1_g[l], ln1_b[l], w1[l], b1[l], w2[l], b2[l],
              ln2_g[l], ln2_b[l])
        xp, conv_p, k_p, v_p = trunk_layer(xp, c_prompt, conv_zero, moba_prompt, *wl)
        ck_l, cv_l = cache_k[l], cache_v[l]

        def attend_sample(q, k, v, ck_l=ck_l, cv_l=cv_l):
            return moba_sample(q, k, v, ck_l, cv_l, page_table)

        xs, conv_s, k_s, v_s = trunk_layer(xs, c_sample, state_conv[l], attend_sample, *wl)
        kp.append(k_p); vp.append(v_p); cp.append(conv_p)
        ksm.append(k_s); vsm.append(v_s); csm.append(conv_s)
    return (xp, xs, jnp.stack(kp), jnp.stack(vp), jnp.stack(cp), jnp.stack(ksm), jnp.stack(vsm), jnp.stack(csm))
```

```python
import functools

import jax
import jax.numpy as jnp
from jax import lax
from jax.experimental import pallas as pl
from jax.experimental.pallas import tpu as pltpu

F32 = jnp.float32
BF16 = jnp.bfloat16

LN_EPS = 1e-5
HEAD_DIM = 64
MOBA_BLOCK = 256
MOBA_TOPK = 3
CONV_WIDTH = 31
PAGE_SIZE = 128
N_MOD = 6
SH1, SC1, G1, SH2, SC2, G2 = range(N_MOD)

SUBLANES = 8
LANES = 128
CONV_HALO = 32
CONV_ROWS = 64
NEG_BIG = -1e30
VMEM_LIMIT = 56 * 1024 * 1024

PROMPT_ROWS = 512
SAMPLE_SEQS = 32


def _cparams(*sem):
    return pltpu.CompilerParams(dimension_semantics=sem, vmem_limit_bytes=VMEM_LIMIT)


def _layer_norm(x, g, b):
    mu = jnp.mean(x, axis=-1, keepdims=True)
    xc = x - mu
    var = jnp.mean(xc * xc, axis=-1, keepdims=True)
    return xc * lax.rsqrt(var + LN_EPS) * g + b


def _rms_scale(x, g):
    return x * lax.rsqrt(jnp.mean(x * x, axis=-1, keepdims=True) + LN_EPS) * g


def _dot(a, b):
    return jnp.dot(a, b, preferred_element_type=F32)


def _dot_t(a, b, precision=None):
    return lax.dot_general(a, b, (((1,), (1,)), ((), ())), preferred_element_type=F32, precision=precision)


def _ln_kernel(x_ref, g_ref, b_ref, o_ref):
    o_ref[...] = _layer_norm(x_ref[...], g_ref[...], b_ref[...])


def _input_ln(x2, g, b, rows):
    n, d = x2.shape
    return pl.pallas_call(
        _ln_kernel,
        grid=(n // rows,),
        in_specs=[pl.BlockSpec((rows, d), lambda i: (i, 0)),
                  pl.BlockSpec((1, d), lambda i: (0, 0)),
                  pl.BlockSpec((1, d), lambda i: (0, 0))],
        out_specs=pl.BlockSpec((rows, d), lambda i: (i, 0)),
        out_shape=jax.ShapeDtypeStruct((n, d), F32),
        compiler_params=_cparams("parallel"),
        name="input_ln",
    )(x2, g.reshape(1, d), b.reshape(1, d))


def _ada_kernel(c_ref, w_ref, b_ref, o_ref):
    c = c_ref[...]
    h = (c * jax.nn.sigmoid(c)).astype(BF16)
    o_ref[0] = _dot(h, w_ref[0]) + b_ref[0]


def _ada(c_all, w_ada_bf, b_ada):
    depth, d, _ = w_ada_bf.shape
    r = c_all.shape[0]
    out = pl.pallas_call(
        _ada_kernel,
        grid=(depth, N_MOD),
        in_specs=[pl.BlockSpec((r, d), lambda l, j: (0, 0)),
                  pl.BlockSpec((1, d, d), lambda l, j: (l, 0, j)),
                  pl.BlockSpec((1, 1, d), lambda l, j: (l * N_MOD + j, 0, 0))],
        out_specs=pl.BlockSpec((1, r, d), lambda l, j: (l * N_MOD + j, 0, 0)),
        out_shape=jax.ShapeDtypeStruct((depth * N_MOD, r, d), F32),
        compiler_params=_cparams("parallel", "parallel"),
        name="ada_mod",
    )(c_all, w_ada_bf, b_ada.reshape(depth * N_MOD, 1, d))
    return out.reshape(depth * N_MOD, r, 1, d)


def _conv_post(y, clg, clb, ogc):
    yn = _layer_norm(y, clg, clb)
    ys = yn * jax.nn.sigmoid(yn)
    return _rms_scale(ys, ogc).astype(BF16)


def _inproj_prompt_kernel(x_ref, sc_ref, sh_ref, w_ref, wdw_ref, bdw_ref, clg_ref, clb_ref, ogc_ref,
                          yc_ref, q_ref, k_ref, v_ref, kb_ref, vb_ref, km_ref, cn_ref, ubuf,
                          *, tm, dc, da):
    i = pl.program_id(1)

    @pl.when(i == 0)
    def _():
        ubuf[0:CONV_HALO, :] = jnp.zeros((CONV_HALO, dc), F32)

    x = x_ref[0]
    h = (x * (1.0 + sc_ref[0, 0]) + sh_ref[0, 0]).astype(BF16)

    zc = _dot(h, w_ref[0, :, 0:2 * dc])
    ubuf[CONV_HALO:CONV_HALO + tm, :] = zc[:, :dc] * jax.nn.sigmoid(zc[:, dc:])

    first = CONV_HALO - (CONV_WIDTH - 1)
    for c in range(tm // CONV_ROWS):
        acc = jnp.broadcast_to(bdw_ref[0], (CONV_ROWS, dc))
        for j in range(CONV_WIDTH):
            r0 = c * CONV_ROWS + first + j
            acc = acc + wdw_ref[0, j:j + 1, :] * ubuf[r0:r0 + CONV_ROWS, :]
        yc_ref[c * CONV_ROWS:(c + 1) * CONV_ROWS, :] = _conv_post(acc, clg_ref[0], clb_ref[0], ogc_ref[0])

    @pl.when(i == pl.num_programs(1) - 1)
    def _():
        cn_ref[0] = ubuf[tm:tm + CONV_HALO, :]

    ubuf[0:CONV_HALO, :] = ubuf[tm:tm + CONV_HALO, :]

    c0 = 2 * dc
    q_ref[...] = _dot(h, w_ref[0, :, c0:c0 + da]) * (HEAD_DIM ** -0.5)
    zk = _dot(h, w_ref[0, :, c0 + da:c0 + 2 * da])
    k_ref[...] = zk
    kb_ref[...] = zk.astype(BF16)
    for r in range(tm // MOBA_BLOCK):
        km_ref[0, r:r + 1, :] = jnp.mean(zk[r * MOBA_BLOCK:(r + 1) * MOBA_BLOCK], axis=0, keepdims=True)
    zv = _dot(h, w_ref[0, :, c0 + 2 * da:c0 + 3 * da])
    v_ref[...] = zv
    vb_ref[...] = zv.astype(BF16)


def _inproj_prompt(l, x3, mod, w_in_bf, w_dw, b_dw3, clg3, clb3, ogc3, tm):
    b, s, d = x3.shape
    dc = w_dw.shape[-1]
    da = (w_in_bf.shape[-1] - 2 * dc) // 3
    r = mod.shape[1]
    nt = s // tm
    nbt = tm // MOBA_BLOCK
    n = b * s
    prow = r - SUBLANES
    kern = functools.partial(_inproj_prompt_kernel, tm=tm, dc=dc, da=da)

    def vec(a):
        return pl.BlockSpec((1, 1, a.shape[-1]), lambda bi, i: (l, 0, 0))

    def rows(width):
        return pl.BlockSpec((tm, width), lambda bi, i: (bi * nt + i, 0))

    outs = pl.pallas_call(
        kern,
        grid=(b, nt),
        in_specs=[pl.BlockSpec((1, tm, d), lambda bi, i: (bi, i, 0)),
                  pl.BlockSpec((1, 1, 1, d), lambda bi, i: (l * N_MOD + SC1, prow + bi, 0, 0)),
                  pl.BlockSpec((1, 1, 1, d), lambda bi, i: (l * N_MOD + SH1, prow + bi, 0, 0)),
                  pl.BlockSpec((1, d, w_in_bf.shape[-1]), lambda bi, i: (l, 0, 0)),
                  pl.BlockSpec((1, CONV_WIDTH, dc), lambda bi, i: (l, 0, 0)),
                  vec(b_dw3), vec(clg3), vec(clb3), vec(ogc3)],
        out_specs=[rows(dc), rows(da), rows(da), rows(da), rows(da), rows(da),
                   pl.BlockSpec((1, nbt, da), lambda bi, i: (bi * nt + i, 0, 0)),
                   pl.BlockSpec((1, CONV_HALO, dc), lambda bi, i: (bi, 0, 0))],
        out_shape=[jax.ShapeDtypeStruct((n, dc), BF16),
                   jax.ShapeDtypeStruct((n, da), F32),
                   jax.ShapeDtypeStruct((n, da), F32),
                   jax.ShapeDtypeStruct((n, da), F32),
                   jax.ShapeDtypeStruct((n, da), BF16),
                   jax.ShapeDtypeStruct((n, da), BF16),
                   jax.ShapeDtypeStruct((b * nt, nbt, da), F32),
                   jax.ShapeDtypeStruct((b, CONV_HALO, dc), F32)],
        scratch_shapes=[pltpu.VMEM((CONV_HALO + tm, dc), F32)],
        compiler_params=_cparams("parallel", "arbitrary"),
        name=f"inproj_prompt_{l}",
    )(x3, mod, mod, w_in_bf, w_dw, b_dw3, clg3, clb3, ogc3)
    return outs


def _inproj_sample_kernel(x_ref, sc_ref, sh_ref, st_ref, w_ref, wdw_ref, bdw_ref, clg_ref, clb_ref, ogc_ref,
                          yc_ref, q_ref, k_ref, v_ref, cn_ref, uext, *, tb, ln, dc, da):
    d = x_ref.shape[-1]
    h = (x_ref[...] * (1.0 + sc_ref[0]) + sh_ref[0]).reshape(tb * ln, d).astype(BF16)

    zc = _dot(h, w_ref[0, :, 0:2 * dc])
    u = zc[:, :dc] * jax.nn.sigmoid(zc[:, dc:])
    uext[:, 0:CONV_HALO, :] = st_ref[...]
    uext[:, CONV_HALO:CONV_HALO + ln, :] = u.reshape(tb, ln, dc)

    first = CONV_HALO - (CONV_WIDTH - 1)
    acc = jnp.broadcast_to(bdw_ref[0], (tb, ln, dc))
    for j in range(CONV_WIDTH):
        acc = acc + wdw_ref[0, j:j + 1, :] * uext[:, first + j:first + j + ln, :]
    yc_ref[...] = _conv_post(acc.reshape(tb * ln, dc), clg_ref[0], clb_ref[0], ogc_ref[0])
    cn_ref[...] = uext[:, ln:ln + CONV_HALO, :]

    c0 = 2 * dc
    q_ref[...] = _dot(h, w_ref[0, :, c0:c0 + da]) * (HEAD_DIM ** -0.5)
    k_ref[...] = _dot(h, w_ref[0, :, c0 + da:c0 + 2 * da])
    v_ref[...] = _dot(h, w_ref[0, :, c0 + 2 * da:c0 + 3 * da])


def _inproj_sample(l, x3, mod, state_pad, w_in_bf, w_dw, b_dw3, clg3, clb3, ogc3, tb):
    bd, ln, d = x3.shape
    dc = w_dw.shape[-1]
    da = (w_in_bf.shape[-1] - 2 * dc) // 3
    n = bd * ln
    kern = functools.partial(_inproj_sample_kernel, tb=tb, ln=ln, dc=dc, da=da)

    def vec(a):
        return pl.BlockSpec((1, 1, a.shape[-1]), lambda i: (l, 0, 0))

    def rows(width):
        return pl.BlockSpec((tb * ln, width), lambda i: (i, 0))

    return pl.pallas_call(
        kern,
        grid=(bd // tb,),
        in_specs=[pl.BlockSpec((tb, ln, d), lambda i: (i, 0, 0)),
                  pl.BlockSpec((1, tb, 1, d), lambda i: (l * N_MOD + SC1, i, 0, 0)),
                  pl.BlockSpec((1, tb, 1, d), lambda i: (l * N_MOD + SH1, i, 0, 0)),
                  pl.BlockSpec((None, tb, CONV_HALO, dc), lambda i: (l, i, 0, 0)),
                  pl.BlockSpec((1, d, w_in_bf.shape[-1]), lambda i: (l, 0, 0)),
                  pl.BlockSpec((1, CONV_WIDTH, dc), lambda i: (l, 0, 0)),
                  vec(b_dw3), vec(clg3), vec(clb3), vec(ogc3)],
        out_specs=[rows(dc), rows(da), rows(da), rows(da),
                   pl.BlockSpec((tb, CONV_HALO, dc), lambda i: (i, 0, 0))],
        out_shape=[jax.ShapeDtypeStruct((n, dc), BF16),
                   jax.ShapeDtypeStruct((n, da), F32),
                   jax.ShapeDtypeStruct((n, da), F32),
                   jax.ShapeDtypeStruct((n, da), F32),
                   jax.ShapeDtypeStruct((bd, CONV_HALO, dc), F32)],
        scratch_shapes=[pltpu.VMEM((tb, CONV_HALO + ln, dc), F32)],
        compiler_params=_cparams("parallel"),
        name=f"inproj_sample_{l}",
    )(x3, mod, mod, state_pad, w_in_bf, w_dw, b_dw3, clg3, clb3, ogc3)


def _topk_lanes(gate, valid, lane, k):
    g = jnp.where(valid, gate, -jnp.inf)
    lane_f = lane.astype(F32)
    sel = jnp.zeros(gate.shape, jnp.bool_)
    for _ in range(k):
        mx = jnp.max(g, axis=-1, keepdims=True)
        idx = jnp.min(jnp.where(g == mx, lane_f, float(gate.shape[-1])), axis=-1, keepdims=True)
        pick = (lane_f == idx) & (mx > -jnp.inf)
        sel = sel | pick
        g = jnp.where(pick, -jnp.inf, g)
    return sel


def _attn_prompt_kernel(q_ref, kb_ref, vb_ref, km_ref, o_ref, kaug, m_sc, l_sc, acc_sc, *, nb):
    j = pl.program_id(2)
    tq = MOBA_BLOCK
    lane = lax.broadcasted_iota(jnp.int32, (tq, LANES), 1)
    head_lanes = (lane < HEAD_DIM, lane >= HEAD_DIM)
    pen_base = (HEAD_DIM, 0)

    @pl.when(j == 0)
    def _():
        for n in range(nb):
            kblk = kb_ref[n * tq:(n + 1) * tq, :]
            for hh in range(2):
                onehot = jnp.where(lane == pen_base[hh] + n, 1.0, 0.0).astype(BF16)
                kaug[hh, n * tq:(n + 1) * tq, :] = jnp.where(head_lanes[hh], kblk, onehot)

    q = q_ref[...]
    km = km_ref[0]
    zpad = LANES - nb
    row_i = lax.broadcasted_iota(jnp.int32, (tq, tq), 0)
    col_i = lax.broadcasted_iota(jnp.int32, (tq, tq), 1)
    lane_nb = lax.broadcasted_iota(jnp.int32, (nb, LANES), 1)

    q_aug = []
    for hh in range(2):
        hl_nb = (lane_nb < HEAD_DIM) if hh == 0 else (lane_nb >= HEAD_DIM)
        kmh = jnp.where(hl_nb, km, 0.0)
        pieces = [jnp.zeros((pen_base[hh], LANES), F32)] if pen_base[hh] else []
        pieces.append(kmh)
        pieces.append(jnp.zeros((zpad - pen_base[hh], LANES), F32))
        kmpad = jnp.concatenate(pieces, axis=0)
        qh = jnp.where(head_lanes[hh], q, 0.0)
        gate = _dot_t(qh, kmpad, precision=lax.Precision.HIGHEST)
        past = (lane >= pen_base[hh]) & (lane < pen_base[hh] + j)
        sel = _topk_lanes(gate, past, lane, min(MOBA_TOPK, nb))
        pen_lanes = (lane >= pen_base[hh]) & (lane < pen_base[hh] + nb)
        open_lane = sel | (lane == pen_base[hh] + j)
        pen = jnp.where(pen_lanes & jnp.logical_not(open_lane), NEG_BIG, 0.0)
        q_aug.append(jnp.where(head_lanes[hh], q, pen).astype(BF16))

    r0 = pl.multiple_of(j * tq, tq)
    v_own = vb_ref[pl.ds(r0, tq), :]
    for hh in range(2):
        s = _dot_t(q_aug[hh], kaug[hh, pl.ds(r0, tq), :])
        s = jnp.where(col_i <= row_i, s, NEG_BIG)
        m = jnp.max(s, axis=-1, keepdims=True)
        p = jnp.exp(s - m)
        m_sc[hh] = jnp.broadcast_to(m, (tq, LANES))
        l_sc[hh] = jnp.broadcast_to(jnp.sum(p, axis=-1, keepdims=True), (tq, LANES))
        acc_sc[hh] = _dot(p.astype(BF16), v_own)

    def body(n, carry):
        k0 = pl.multiple_of(n * tq, tq)
        vblk = vb_ref[pl.ds(k0, tq), :]
        for hh in range(2):
            s = _dot_t(q_aug[hh], kaug[hh, pl.ds(k0, tq), :])
            m_prev = m_sc[hh]
            m_new = jnp.maximum(m_prev, jnp.max(s, axis=-1, keepdims=True))
            alpha = jnp.exp(m_prev - m_new)
            p = jnp.exp(s - jnp.concatenate([m_new, m_new], axis=1))
            l_sc[hh] = alpha * l_sc[hh] + jnp.sum(p, axis=-1, keepdims=True)
            acc_sc[hh] = alpha * acc_sc[hh] + _dot(p.astype(BF16), vblk)
            m_sc[hh] = m_new
        return carry

    lax.fori_loop(0, j, body, 0)

    o0 = acc_sc[0] / l_sc[0]
    o1 = acc_sc[1] / l_sc[1]
    o_ref[...] = jnp.where(head_lanes[0], o0, o1)


def _attn_prompt(l, q2, kb2, vb2, km3, b, s):
    n, da = q2.shape
    nb = s // MOBA_BLOCK
    hp = da // LANES
    tq = MOBA_BLOCK
    km3 = km3.reshape(b, nb, da)
    kern = functools.partial(_attn_prompt_kernel, nb=nb)
    return pl.pallas_call(
        kern,
        grid=(b, hp, nb),
        in_specs=[pl.BlockSpec((tq, LANES), lambda bi, p, j: (bi * nb + j, p)),
                  pl.BlockSpec((s, LANES), lambda bi, p, j: (bi, p)),
                  pl.BlockSpec((s, LANES), lambda bi, p, j: (bi, p)),
                  pl.BlockSpec((1, nb, LANES), lambda bi, p, j: (bi, 0, p))],
        out_specs=pl.BlockSpec((tq, LANES), lambda bi, p, j: (bi * nb + j, p)),
        out_shape=jax.ShapeDtypeStruct((n, da), F32),
        scratch_shapes=[pltpu.VMEM((2, s, LANES), BF16),
                        pltpu.VMEM((2, tq, LANES), F32),
                        pltpu.VMEM((2, tq, LANES), F32),
                        pltpu.VMEM((2, tq, LANES), F32)],
        compiler_params=_cparams("parallel", "parallel", "arbitrary"),
        name=f"attn_prompt_{l}",
    )(q2, kb2, vb2, km3)


def _attn_sample_kernel(pt_ref, q_ref, kn_ref, vn_ref, *refs, n_pages, nh, ln):
    del pt_ref
    k_pages = refs[:n_pages]
    v_pages = refs[n_pages:2 * n_pages]
    o_ref, kbuf, vbuf, km_sc = refs[2 * n_pages:]
    da = nh * HEAD_DIM
    rows = nh * ln
    ppb = MOBA_BLOCK // PAGE_SIZE
    nbp = n_pages // ppb

    q = q_ref[0]
    qt = jnp.concatenate([q] * nh, axis=0)
    row = lax.broadcasted_iota(jnp.int32, (rows, da), 0)
    lane = lax.broadcasted_iota(jnp.int32, (rows, da), 1)
    own_head = (lane // HEAD_DIM) == (row // ln)
    qbd = jnp.where(own_head, qt, 0.0)
    qbd_bf = qbd.astype(BF16)

    for n in range(nbp):
        ssum = jnp.zeros((1, da), F32)
        for pg in range(ppb):
            p_i = n * ppb + pg
            kp = k_pages[p_i][0]
            ssum = ssum + jnp.sum(kp, axis=0, keepdims=True)
            kbuf[p_i * PAGE_SIZE:(p_i + 1) * PAGE_SIZE, :] = kp.astype(BF16)
            vbuf[p_i * PAGE_SIZE:(p_i + 1) * PAGE_SIZE, :] = v_pages[p_i][0].astype(BF16)
        km_sc[n:n + 1, :] = ssum * (1.0 / MOBA_BLOCK)

    gate = _dot_t(qbd, km_sc[...], precision=lax.Precision.HIGHEST)
    glane = lax.broadcasted_iota(jnp.int32, (rows, nbp), 1)
    sel = _topk_lanes(gate, glane >= 0, glane, min(MOBA_TOPK, nbp))
    bias = jnp.where(sel, 0.0, NEG_BIG)

    zrows = jnp.zeros((ln, da), F32)
    kn = jnp.concatenate([kn_ref[0], zrows], axis=0).astype(BF16)
    vn = jnp.concatenate([vn_ref[0], zrows], axis=0).astype(BF16)
    s_own = _dot_t(qbd_bf, kn)
    orow = lax.broadcasted_iota(jnp.int32, (rows, 2 * ln), 0)
    ocol = lax.broadcasted_iota(jnp.int32, (rows, 2 * ln), 1)
    s_own = jnp.where(ocol <= orow % ln, s_own, NEG_BIG)
    m = jnp.max(s_own, axis=-1, keepdims=True)

    s_past = []
    for n in range(nbp):
        s = _dot_t(qbd_bf, kbuf[n * MOBA_BLOCK:(n + 1) * MOBA_BLOCK, :]) + bias[:, n:n + 1]
        s_past.append(s)
        m = jnp.maximum(m, jnp.max(s, axis=-1, keepdims=True))

    p_own = jnp.exp(s_own - m)
    lsum = jnp.sum(p_own, axis=-1, keepdims=True)
    acc = _dot(p_own.astype(BF16), vn)
    for n in range(nbp):
        p = jnp.exp(s_past[n] - m)
        lsum = lsum + jnp.sum(p, axis=-1, keepdims=True)
        acc = acc + _dot(p.astype(BF16), vbuf[n * MOBA_BLOCK:(n + 1) * MOBA_BLOCK, :])

    o = jnp.where(own_head, acc / lsum, 0.0)
    out = o[0:ln]
    for hh in range(1, nh):
        out = out + o[hh * ln:(hh + 1) * ln]
    o_ref[0] = out


def _attn_sample(l, q3, k3, v3, ck2, cv2, pt_flat, n_pool, n_pages):
    bd, ln, da = q3.shape
    nh = da // HEAD_DIM
    kern = functools.partial(_attn_sample_kernel, n_pages=n_pages, nh=nh, ln=ln)

    def page_spec(p_i):
        return pl.BlockSpec((1, PAGE_SIZE, da), lambda i, pt: (l * n_pool + pt[i * n_pages + p_i], 0, 0))

    tok = pl.BlockSpec((1, ln, da), lambda i, pt: (i, 0, 0))
    grid_spec = pltpu.PrefetchScalarGridSpec(
        num_scalar_prefetch=1,
        grid=(bd,),
        in_specs=[tok, tok, tok] + [page_spec(p_i) for p_i in range(n_pages)] * 2,
        out_specs=tok,
        scratch_shapes=[pltpu.VMEM((n_pages * PAGE_SIZE, da), BF16),
                        pltpu.VMEM((n_pages * PAGE_SIZE, da), BF16),
                        pltpu.VMEM((n_pages * PAGE_SIZE // MOBA_BLOCK, da), F32)],
    )
    return pl.pallas_call(
        kern,
        grid_spec=grid_spec,
        out_shape=jax.ShapeDtypeStruct((bd, ln, da), F32),
        compiler_params=_cparams("arbitrary"),
        name=f"attn_sample_{l}",
    )(pt_flat, q3, k3, v3, *([ck2] * n_pages), *([cv2] * n_pages))


def _outmlp_kernel(x_ref, yc_ref, ya_ref, g1_ref, sc2_ref, sh2_ref, g2_ref, oga_ref, wout_ref,
                   ln1g_ref, ln1b_ref, w1_ref, b1_ref, w2_ref, b2_ref, ln2g_ref, ln2b_ref, o_ref,
                   *, alpha, ff_chunk):
    x = x_ref[...]
    blk = x.shape
    rows, d = blk[0] * blk[1], blk[2]
    dc = yc_ref.shape[-1]
    dff = w1_ref.shape[-1]

    yan = _rms_scale(ya_ref[...], oga_ref[0]).astype(BF16)
    mix = _dot(yc_ref[...], wout_ref[0, 0:dc, :]) + _dot(yan, wout_ref[0, dc:, :])
    x1 = _layer_norm(alpha * x + (1.0 + g1_ref[0]) * mix.reshape(blk), ln1g_ref[0], ln1b_ref[0])

    h2 = (x1 * (1.0 + sc2_ref[0]) + sh2_ref[0]).reshape(rows, d).astype(BF16)
    f = jnp.broadcast_to(b2_ref[0], (rows, d))
    for c in range(dff // ff_chunk):
        cs = slice(c * ff_chunk, (c + 1) * ff_chunk)
        hid = jnp.maximum(_dot(h2, w1_ref[0, :, cs]) + b1_ref[0, :, cs], 0.0)
        f = f + _dot((hid * hid).astype(BF16), w2_ref[0, cs, :])
    o_ref[...] = _layer_norm(alpha * x1 + (1.0 + g2_ref[0]) * f.reshape(blk), ln2g_ref[0], ln2b_ref[0])


def _outmlp(l, x3, yc2, ya2, mod, mod_row0, per_row_mod, oga3, w_out_bf, ln1g3, ln1b3, w1_bf, b13, w2_bf,
            b23, ln2g3, ln2b3, alpha, grp, tag):
    a, r, d = x3.shape
    dc, da = yc2.shape[-1], ya2.shape[-1]
    dff = w1_bf.shape[-1]
    if per_row_mod:
        blk = (grp, r, d)
        grid = (a // grp,)
        xmap = lambda i: (i, 0, 0)
        rmap = lambda i: (i, 0)
        brows = grp * r
        mblk = (1, grp, 1, d)

        def mod_spec(comp):
            return pl.BlockSpec(mblk, lambda i: (l * N_MOD + comp, i, 0, 0))
        sem = ("parallel",)
    else:
        nt = r // grp
        blk = (1, grp, d)
        grid = (a, nt)
        xmap = lambda bi, i: (bi, i, 0)
        rmap = lambda bi, i: (bi * nt + i, 0)
        brows = grp
        mblk = (1, 1, 1, d)

        def mod_spec(comp):
            return pl.BlockSpec(mblk, lambda bi, i: (l * N_MOD + comp, mod_row0 + bi, 0, 0))
        sem = ("parallel", "parallel")

    nargs = len(grid)

    def const(shape):
        zeros = (0,) * (len(shape) - 1)
        if nargs == 1:
            return pl.BlockSpec(shape, lambda i: (l,) + zeros, pipeline_mode=pl.Buffered(1))
        return pl.BlockSpec(shape, lambda bi, i: (l,) + zeros, pipeline_mode=pl.Buffered(1))

    kern = functools.partial(_outmlp_kernel, alpha=alpha, ff_chunk=min(dff, 1024))
    return pl.pallas_call(
        kern,
        grid=grid,
        in_specs=[pl.BlockSpec(blk, xmap),
                  pl.BlockSpec((brows, dc), rmap),
                  pl.BlockSpec((brows, da), rmap),
                  mod_spec(G1), mod_spec(SC2), mod_spec(SH2), mod_spec(G2),
                  const((1, 1, da)), const((1, dc + da, d)),
                  const((1, 1, d)), const((1, 1, d)),
                  const((1, d, dff)), const((1, 1, dff)), const((1, dff, d)), const((1, 1, d)),
                  const((1, 1, d)), const((1, 1, d))],
        out_specs=pl.BlockSpec(blk, xmap),
        out_shape=jax.ShapeDtypeStruct((a, r, d), F32),
        compiler_params=_cparams(*sem),
        name=f"outmlp_{tag}_{l}",
    )(x3, yc2, ya2, mod, mod, mod, mod, oga3, w_out_bf, ln1g3, ln1b3, w1_bf, b13, w2_bf, b23, ln2g3, ln2b3)


def kernel(x_prompt, x_sample, cache_k, cache_v, state_conv, page_table, c_prompt, c_sample, ln0_g, ln0_b, w_ada, b_ada, w_in, w_dw, b_dw, conv_ln_g, conv_ln_b, out_g_conv, out_g_attn, w_out, ln1_g, ln1_b, w1, b1, w2, b2, ln2_g, ln2_b):
    b, s, d = x_prompt.shape
    bd, ln, _ = x_sample.shape
    depth = w_in.shape[0]
    dc = w_dw.shape[-1]
    da = out_g_attn.shape[-1]
    nh = da // HEAD_DIM
    n_pool = cache_k.shape[1]
    n_pages = page_table.shape[1]
    alpha = (2 * depth) ** 0.25
    assert (n_pages * PAGE_SIZE) % MOBA_BLOCK == 0 and ln <= MOBA_BLOCK and ln == SUBLANES
    assert s % MOBA_BLOCK == 0 and s // MOBA_BLOCK <= HEAD_DIM and da % LANES == 0
    tm = min(PROMPT_ROWS, s)
    tb = min(SAMPLE_SEQS, bd)

    w_ada_bf, w_in_bf, w_out_bf = w_ada.astype(BF16), w_in.astype(BF16), w_out.astype(BF16)
    w1_bf, w2_bf = w1.astype(BF16), w2.astype(BF16)

    def vec3(a):
        return a.reshape(depth, 1, a.shape[-1])

    b_dw3, clg3, clb3, ogc3, oga3 = vec3(b_dw), vec3(conv_ln_g), vec3(conv_ln_b), vec3(out_g_conv), vec3(out_g_attn)
    ln1g3, ln1b3, ln2g3, ln2b3, b13, b23 = vec3(ln1_g), vec3(ln1_b), vec3(ln2_g), vec3(ln2_b), vec3(b1), vec3(b2)

    n_rows = bd + SUBLANES
    assert b <= SUBLANES and bd % SUBLANES == 0
    c_all = jnp.concatenate([c_sample, c_prompt, jnp.zeros((SUBLANES - b, d), F32)], axis=0)
    mod = _ada(c_all, w_ada_bf, b_ada)
    mod_row0 = n_rows - SUBLANES

    xp = _input_ln(x_prompt.reshape(b * s, d), ln0_g, ln0_b, tm).reshape(b, s, d)
    xs = _input_ln(x_sample.reshape(bd * ln, d), ln0_g, ln0_b, min(bd * ln, 512)).reshape(bd, ln, d)

    state_pad = jnp.pad(state_conv, ((0, 0), (0, 0), (CONV_HALO - (CONV_WIDTH - 1), 0), (0, 0)))
    ck2 = cache_k.reshape(depth * n_pool, PAGE_SIZE, da)
    cv2 = cache_v.reshape(depth * n_pool, PAGE_SIZE, da)
    pt_flat = page_table.reshape(-1).astype(jnp.int32)

    kp, vp, cp, ksm, vsm, csm = [], [], [], [], [], []
    for l in range(depth):
        yc, q2, k2, v2, kb2, vb2, km3, cn = _inproj_prompt(l, xp, mod, w_in_bf, w_dw, b_dw3, clg3, clb3, ogc3, tm)
        ya = _attn_prompt(l, q2, kb2, vb2, km3, b, s)
        xp = _outmlp(l, xp, yc, ya, mod, mod_row0, False, oga3, w_out_bf, ln1g3, ln1b3, w1_bf, b13, w2_bf, b23,
                     ln2g3, ln2b3, alpha, tm, "prompt")
        kp.append(k2.reshape(b, s, nh, HEAD_DIM))
        vp.append(v2.reshape(b, s, nh, HEAD_DIM))
        cp.append(cn[:, CONV_HALO - (CONV_WIDTH - 1):])

        ycs, qs, ks, vs, cns = _inproj_sample(l, xs, mod, state_pad, w_in_bf, w_dw, b_dw3, clg3, clb3, ogc3, tb)
        yas = _attn_sample(l, qs.reshape(bd, ln, da), ks.reshape(bd, ln, da), vs.reshape(bd, ln, da),
                           ck2, cv2, pt_flat, n_pool, n_pages)
        xs = _outmlp(l, xs, ycs, yas.reshape(bd * ln, da), mod, 0, True, oga3, w_out_bf, ln1g3, ln1b3, w1_bf,
                     b13, w2_bf, b23, ln2g3, ln2b3, alpha, tb, "sample")
        ksm.append(ks.reshape(bd, ln, nh, HEAD_DIM))
        vsm.append(vs.reshape(bd, ln, nh, HEAD_DIM))
        csm.append(cns[:, CONV_HALO - (CONV_WIDTH - 1):])

    return (xp, xs, jnp.stack(kp), jnp.stack(vp), jnp.stack(cp), jnp.stack(ksm), jnp.stack(vsm), jnp.stack(csm))
```

```python
import functools

import jax
import jax.numpy as jnp
from jax import lax
from jax.experimental import pallas as pl
from jax.experimental.pallas import tpu as pltpu

F32 = jnp.float32
BF16 = jnp.bfloat16

LN_EPS = 1e-5
HEAD_DIM = 64
MOBA_BLOCK = 256
MOBA_TOPK = 3
CONV_WIDTH = 31
PAGE_SIZE = 128
N_MOD = 6
SH1, SC1, G1, SH2, SC2, G2 = range(N_MOD)

SUBLANES = 8
LANES = 128
CONV_HALO = 32
CONV_ROWS = 64
NEG_BIG = -1e30
VMEM_LIMIT = 56 * 1024 * 1024

PROMPT_ROWS = 512
SAMPLE_SEQS = 32
KV_TILE_BLOCKS = 4


def _cparams(*sem):
    return pltpu.CompilerParams(dimension_semantics=sem, vmem_limit_bytes=VMEM_LIMIT)


def _layer_norm(x, g, b):
    mu = jnp.mean(x, axis=-1, keepdims=True)
    xc = x - mu
    var = jnp.mean(xc * xc, axis=-1, keepdims=True)
    return xc * lax.rsqrt(var + LN_EPS) * g + b


def _rms_scale(x, g):
    return x * lax.rsqrt(jnp.mean(x * x, axis=-1, keepdims=True) + LN_EPS) * g


def _dot(a, b):
    return jnp.dot(a, b, preferred_element_type=F32)


def _dot_t(a, b, precision=None):
    return lax.dot_general(a, b, (((1,), (1,)), ((), ())), preferred_element_type=F32, precision=precision)


def _ln_kernel(x_ref, g_ref, b_ref, o_ref):
    o_ref[...] = _layer_norm(x_ref[...], g_ref[...], b_ref[...])


def _input_ln(x2, g, b, rows):
    n, d = x2.shape
    return pl.pallas_call(
        _ln_kernel,
        grid=(n // rows,),
        in_specs=[pl.BlockSpec((rows, d), lambda i: (i, 0)),
                  pl.BlockSpec((1, d), lambda i: (0, 0)),
                  pl.BlockSpec((1, d), lambda i: (0, 0))],
        out_specs=pl.BlockSpec((rows, d), lambda i: (i, 0)),
        out_shape=jax.ShapeDtypeStruct((n, d), F32),
        compiler_params=_cparams("parallel"),
        name="input_ln",
    )(x2, g.reshape(1, d), b.reshape(1, d))


def _ada_kernel(c_ref, w_ref, b_ref, o_ref):
    c = c_ref[...]
    h = (c * jax.nn.sigmoid(c)).astype(BF16)
    o_ref[0] = _dot(h, w_ref[0]) + b_ref[0]


def _ada(c_all, w_ada_bf, b_ada):
    depth, d, _ = w_ada_bf.shape
    r = c_all.shape[0]
    out = pl.pallas_call(
        _ada_kernel,
        grid=(depth, N_MOD),
        in_specs=[pl.BlockSpec((r, d), lambda l, j: (0, 0)),
                  pl.BlockSpec((1, d, d), lambda l, j: (l, 0, j)),
                  pl.BlockSpec((1, 1, d), lambda l, j: (l * N_MOD + j, 0, 0))],
        out_specs=pl.BlockSpec((1, r, d), lambda l, j: (l * N_MOD + j, 0, 0)),
        out_shape=jax.ShapeDtypeStruct((depth * N_MOD, r, d), F32),
        compiler_params=_cparams("parallel", "parallel"),
        name="ada_mod",
    )(c_all, w_ada_bf, b_ada.reshape(depth * N_MOD, 1, d))
    return out.reshape(depth * N_MOD, r, 1, d)


def _conv_post(y, clg, clb, ogc):
    yn = _layer_norm(y, clg, clb)
    ys = yn * jax.nn.sigmoid(yn)
    return _rms_scale(ys, ogc).astype(BF16)


def _inproj_prompt_kernel(x_ref, sc_ref, sh_ref, w_ref, wdw_ref, bdw_ref, clg_ref, clb_ref, ogc_ref,
                          yc_ref, q_ref, k_ref, v_ref, kb_ref, vb_ref, km_ref, cn_ref, ubuf, ushift,
                          *, tm, dc, da):
    i = pl.program_id(1)

    @pl.when(i == 0)
    def _():
        ubuf[0:CONV_HALO, :] = jnp.zeros((CONV_HALO, dc), F32)

    x = x_ref[0]
    h = (x * (1.0 + sc_ref[0, 0]) + sh_ref[0, 0]).astype(BF16)

    zc = _dot(h, w_ref[0, :, 0:2 * dc])
    ubuf[CONV_HALO:CONV_HALO + tm, :] = zc[:, :dc] * jax.nn.sigmoid(zc[:, dc:])

    first = CONV_HALO - (CONV_WIDTH - 1)
    span = tm + CONV_HALO - SUBLANES
    for r in range(1, SUBLANES):
        ushift[r - 1] = ubuf[r:r + span, :]
    for c in range(tm // CONV_ROWS):
        acc = jnp.broadcast_to(bdw_ref[0], (CONV_ROWS, dc))
        for j in range(CONV_WIDTH):
            a, r = divmod(first + j, SUBLANES)
            r0 = c * CONV_ROWS + a * SUBLANES
            src = ubuf[r0:r0 + CONV_ROWS, :] if r == 0 else ushift[r - 1, r0:r0 + CONV_ROWS, :]
            acc = acc + wdw_ref[0, j:j + 1, :] * src
        yc_ref[c * CONV_ROWS:(c + 1) * CONV_ROWS, :] = _conv_post(acc, clg_ref[0], clb_ref[0], ogc_ref[0])

    @pl.when(i == pl.num_programs(1) - 1)
    def _():
        cn_ref[0] = ubuf[tm:tm + CONV_HALO, :]

    ubuf[0:CONV_HALO, :] = ubuf[tm:tm + CONV_HALO, :]

    c0 = 2 * dc
    q_ref[...] = _dot(h, w_ref[0, :, c0:c0 + da]) * (HEAD_DIM ** -0.5)
    zk = _dot(h, w_ref[0, :, c0 + da:c0 + 2 * da])
    k_ref[...] = zk
    kb_ref[...] = zk.astype(BF16)
    for r in range(tm // MOBA_BLOCK):
        km_ref[0, r:r + 1, :] = jnp.mean(zk[r * MOBA_BLOCK:(r + 1) * MOBA_BLOCK], axis=0, keepdims=True)
    zv = _dot(h, w_ref[0, :, c0 + 2 * da:c0 + 3 * da])
    v_ref[...] = zv
    vb_ref[...] = zv.astype(BF16)


def _inproj_prompt(l, x3, mod, w_in_bf, w_dw, b_dw3, clg3, clb3, ogc3, tm):
    b, s, d = x3.shape
    dc = w_dw.shape[-1]
    da = (w_in_bf.shape[-1] - 2 * dc) // 3
    r = mod.shape[1]
    nt = s // tm
    nbt = tm // MOBA_BLOCK
    n = b * s
    prow = r - SUBLANES
    kern = functools.partial(_inproj_prompt_kernel, tm=tm, dc=dc, da=da)

    def vec(a):
        return pl.BlockSpec((1, 1, a.shape[-1]), lambda bi, i: (l, 0, 0))

    def rows(width):
        return pl.BlockSpec((tm, width), lambda bi, i: (bi * nt + i, 0))

    outs = pl.pallas_call(
        kern,
        grid=(b, nt),
        in_specs=[pl.BlockSpec((1, tm, d), lambda bi, i: (bi, i, 0)),
                  pl.BlockSpec((1, 1, 1, d), lambda bi, i: (l * N_MOD + SC1, prow + bi, 0, 0)),
                  pl.BlockSpec((1, 1, 1, d), lambda bi, i: (l * N_MOD + SH1, prow + bi, 0, 0)),
                  pl.BlockSpec((1, d, w_in_bf.shape[-1]), lambda bi, i: (l, 0, 0)),
                  pl.BlockSpec((1, CONV_WIDTH, dc), lambda bi, i: (l, 0, 0)),
                  vec(b_dw3), vec(clg3), vec(clb3), vec(ogc3)],
        out_specs=[rows(dc), rows(da), rows(da), rows(da), rows(da), rows(da),
                   pl.BlockSpec((1, nbt, da), lambda bi, i: (bi * nt + i, 0, 0)),
                   pl.BlockSpec((1, CONV_HALO, dc), lambda bi, i: (bi, 0, 0))],
        out_shape=[jax.ShapeDtypeStruct((n, dc), BF16),
                   jax.ShapeDtypeStruct((n, da), F32),
                   jax.ShapeDtypeStruct((n, da), F32),
                   jax.ShapeDtypeStruct((n, da), F32),
                   jax.ShapeDtypeStruct((n, da), BF16),
                   jax.ShapeDtypeStruct((n, da), BF16),
                   jax.ShapeDtypeStruct((b * nt, nbt, da), F32),
                   jax.ShapeDtypeStruct((b, CONV_HALO, dc), F32)],
        scratch_shapes=[pltpu.VMEM((CONV_HALO + tm, dc), F32),
                        pltpu.VMEM((SUBLANES - 1, tm + CONV_HALO - SUBLANES, dc), F32)],
        compiler_params=_cparams("parallel", "arbitrary"),
        name=f"inproj_prompt_{l}",
    )(x3, mod, mod, w_in_bf, w_dw, b_dw3, clg3, clb3, ogc3)
    return outs


def _inproj_sample_kernel(x_ref, sc_ref, sh_ref, st_ref, w_ref, wdw_ref, bdw_ref, clg_ref, clb_ref, ogc_ref,
                          yc_ref, q_ref, k_ref, v_ref, cn_ref, uext, *, tb, ln, dc, da):
    d = x_ref.shape[-1]
    h = (x_ref[...] * (1.0 + sc_ref[0]) + sh_ref[0]).reshape(tb * ln, d).astype(BF16)

    zc = _dot(h, w_ref[0, :, 0:2 * dc])
    u = zc[:, :dc] * jax.nn.sigmoid(zc[:, dc:])
    uext[:, 0:CONV_HALO, :] = st_ref[...]
    uext[:, CONV_HALO:CONV_HALO + ln, :] = u.reshape(tb, ln, dc)

    first = CONV_HALO - (CONV_WIDTH - 1)
    acc = jnp.broadcast_to(bdw_ref[0], (tb, ln, dc))
    for j in range(CONV_WIDTH):
        acc = acc + wdw_ref[0, j:j + 1, :] * uext[:, first + j:first + j + ln, :]
    yc_ref[...] = _conv_post(acc.reshape(tb * ln, dc), clg_ref[0], clb_ref[0], ogc_ref[0])
    cn_ref[...] = uext[:, ln:ln + CONV_HALO, :]

    c0 = 2 * dc
    q_ref[...] = _dot(h, w_ref[0, :, c0:c0 + da]) * (HEAD_DIM ** -0.5)
    k_ref[...] = _dot(h, w_ref[0, :, c0 + da:c0 + 2 * da])
    v_ref[...] = _dot(h, w_ref[0, :, c0 + 2 * da:c0 + 3 * da])


def _inproj_sample(l, x3, mod, state_pad, w_in_bf, w_dw, b_dw3, clg3, clb3, ogc3, tb):
    bd, ln, d = x3.shape
    dc = w_dw.shape[-1]
    da = (w_in_bf.shape[-1] - 2 * dc) // 3
    n = bd * ln
    kern = functools.partial(_inproj_sample_kernel, tb=tb, ln=ln, dc=dc, da=da)

    def vec(a):
        return pl.BlockSpec((1, 1, a.shape[-1]), lambda i: (l, 0, 0))

    def rows(width):
        return pl.BlockSpec((tb * ln, width), lambda i: (i, 0))

    return pl.pallas_call(
        kern,
        grid=(bd // tb,),
        in_specs=[pl.BlockSpec((tb, ln, d), lambda i: (i, 0, 0)),
                  pl.BlockSpec((1, tb, 1, d), lambda i: (l * N_MOD + SC1, i, 0, 0)),
                  pl.BlockSpec((1, tb, 1, d), lambda i: (l * N_MOD + SH1, i, 0, 0)),
                  pl.BlockSpec((None, tb, CONV_HALO, dc), lambda i: (l, i, 0, 0)),
                  pl.BlockSpec((1, d, w_in_bf.shape[-1]), lambda i: (l, 0, 0)),
                  pl.BlockSpec((1, CONV_WIDTH, dc), lambda i: (l, 0, 0)),
                  vec(b_dw3), vec(clg3), vec(clb3), vec(ogc3)],
        out_specs=[rows(dc), rows(da), rows(da), rows(da),
                   pl.BlockSpec((tb, CONV_HALO, dc), lambda i: (i, 0, 0))],
        out_shape=[jax.ShapeDtypeStruct((n, dc), BF16),
                   jax.ShapeDtypeStruct((n, da), F32),
                   jax.ShapeDtypeStruct((n, da), F32),
                   jax.ShapeDtypeStruct((n, da), F32),
                   jax.ShapeDtypeStruct((bd, CONV_HALO, dc), F32)],
        scratch_shapes=[pltpu.VMEM((tb, CONV_HALO + ln, dc), F32)],
        compiler_params=_cparams("parallel"),
        name=f"inproj_sample_{l}",
    )(x3, mod, mod, state_pad, w_in_bf, w_dw, b_dw3, clg3, clb3, ogc3)


def _topk_mask(gate, valid, pos, k, axis):
    g = jnp.where(valid, gate, -jnp.inf)
    sel = jnp.zeros(gate.shape, jnp.bool_)
    for _ in range(k):
        mx = jnp.max(g, axis=axis, keepdims=True)
        idx = jnp.min(jnp.where(g == mx, pos, float(gate.shape[axis])), axis=axis, keepdims=True)
        pick = (pos == idx) & (mx > -jnp.inf)
        sel = sel | pick
        g = jnp.where(pick, -jnp.inf, g)
    return sel


def _attn_prompt_kernel(q_ref, kb_ref, vb_ref, km_ref, o_ref, kaug, qaug, m_sc, l_sc, acc_sc, *, nb, kvb):
    j = pl.program_id(2)
    tq = MOBA_BLOCK
    kc = kvb * MOBA_BLOCK
    lane = lax.broadcasted_iota(jnp.int32, (tq, LANES), 1)
    head_lanes = (lane < HEAD_DIM, lane >= HEAD_DIM)
    pen_base = (HEAD_DIM, 0)

    @pl.when(j == 0)
    def _():
        for n in range(nb):
            kblk = kb_ref[n * tq:(n + 1) * tq, :]
            for hh in range(2):
                onehot = jnp.where(lane == pen_base[hh] + n, 1.0, 0.0).astype(BF16)
                kaug[hh, n * tq:(n + 1) * tq, :] = jnp.where(head_lanes[hh], kblk, onehot)

    q = q_ref[...]
    km = km_ref[0]
    lane_nb = lax.broadcasted_iota(jnp.int32, (nb, LANES), 1)
    blk_i = lax.broadcasted_iota(jnp.int32, (nb, tq), 0)
    blk_f = blk_i.astype(F32)
    for hh in range(2):
        hl_nb = (lane_nb < HEAD_DIM) if hh == 0 else (lane_nb >= HEAD_DIM)
        kmh = jnp.where(hl_nb, km, 0.0)
        qh = jnp.where(head_lanes[hh], q, 0.0)
        gate_t = _dot_t(kmh, qh, precision=lax.Precision.HIGHEST)
        sel = _topk_mask(gate_t, blk_i < j, blk_f, min(MOBA_TOPK, nb), 0)
        pen_t = jnp.where(sel | (blk_i == j), 0.0, NEG_BIG)
        pieces = [jnp.zeros((pen_base[hh], tq), F32)] if pen_base[hh] else []
        pieces += [pen_t, jnp.zeros((LANES - nb - pen_base[hh], tq), F32)]
        pen = jnp.concatenate(pieces, axis=0).T
        qaug[hh] = jnp.where(head_lanes[hh], q, pen).astype(BF16)

    m_sc[...] = jnp.full(m_sc.shape, -jnp.inf, F32)
    l_sc[...] = jnp.zeros(l_sc.shape, F32)
    acc_sc[...] = jnp.zeros(acc_sc.shape, F32)

    def tile(c, causal):
        k0 = pl.multiple_of(c * kc, kc)
        vblk = vb_ref[pl.ds(k0, kc), :]
        for hh in range(2):
            s = _dot_t(qaug[hh], kaug[hh, pl.ds(k0, kc), :])
            if causal:
                row_i = lax.broadcasted_iota(jnp.int32, (tq, kc), 0)
                col_i = lax.broadcasted_iota(jnp.int32, (tq, kc), 1)
                s = jnp.where(col_i - row_i <= j * tq - k0, s, NEG_BIG)
            m_prev = m_sc[hh]
            m_new = jnp.maximum(m_prev, jnp.max(s, axis=-1, keepdims=True))
            alpha = jnp.exp(m_prev - m_new)
            p = jnp.exp(s - pltpu.repeat(m_new, kc // LANES, axis=1))
            l_sc[hh] = alpha * l_sc[hh] + jnp.sum(p, axis=-1, keepdims=True)
            acc_sc[hh] = alpha * acc_sc[hh] + _dot(p.astype(BF16), vblk)
            m_sc[hh] = m_new

    c_own = j // kvb
    tile(c_own, True)

    def body(i, carry):
        tile(c_own - 1 - i, False)
        return carry

    lax.fori_loop(0, c_own, body, 0)

    o0 = acc_sc[0] / l_sc[0]
    o1 = acc_sc[1] / l_sc[1]
    o_ref[...] = jnp.where(head_lanes[0], o0, o1)


def _attn_prompt(l, q2, kb2, vb2, km3, b, s):
    n, da = q2.shape
    nb = s // MOBA_BLOCK
    hp = da // LANES
    tq = MOBA_BLOCK
    kvb = max(c for c in range(1, KV_TILE_BLOCKS + 1) if nb % c == 0)
    km3 = km3.reshape(b, nb, da)
    kern = functools.partial(_attn_prompt_kernel, nb=nb, kvb=kvb)
    return pl.pallas_call(
        kern,
        grid=(b, hp, nb),
        in_specs=[pl.BlockSpec((tq, LANES), lambda bi, p, j: (bi * nb + j, p)),
                  pl.BlockSpec((s, LANES), lambda bi, p, j: (bi, p)),
                  pl.BlockSpec((s, LANES), lambda bi, p, j: (bi, p)),
                  pl.BlockSpec((1, nb, LANES), lambda bi, p, j: (bi, 0, p))],
        out_specs=pl.BlockSpec((tq, LANES), lambda bi, p, j: (bi * nb + j, p)),
        out_shape=jax.ShapeDtypeStruct((n, da), F32),
        scratch_shapes=[pltpu.VMEM((2, s, LANES), BF16),
                        pltpu.VMEM((2, tq, LANES), BF16),
                        pltpu.VMEM((2, tq, LANES), F32),
                        pltpu.VMEM((2, tq, LANES), F32),
                        pltpu.VMEM((2, tq, LANES), F32)],
        compiler_params=_cparams("parallel", "parallel", "arbitrary"),
        name=f"attn_prompt_{l}",
    )(q2, kb2, vb2, km3)


def _attn_sample_kernel(pt_ref, q_ref, kn_ref, vn_ref, *refs, n_pages, nh, ln):
    del pt_ref
    k_pages = refs[:n_pages]
    v_pages = refs[n_pages:2 * n_pages]
    o_ref, kbuf, vbuf = refs[2 * n_pages:]
    da = nh * HEAD_DIM
    rows = nh * ln
    ppb = MOBA_BLOCK // PAGE_SIZE
    nbp = n_pages // ppb

    q = q_ref[0]
    qt = jnp.concatenate([q] * nh, axis=0)
    row = lax.broadcasted_iota(jnp.int32, (rows, da), 0)
    lane = lax.broadcasted_iota(jnp.int32, (rows, da), 1)
    own_head = (lane // HEAD_DIM) == (row // ln)
    qbd = jnp.where(own_head, qt, 0.0)
    qbd_bf = qbd.astype(BF16)

    km_lane = lax.broadcasted_iota(jnp.int32, (da, LANES), 1)
    km_t = jnp.zeros((da, LANES), F32)
    for n in range(nbp):
        ksum = jnp.zeros((da, PAGE_SIZE), F32)
        for pg in range(ppb):
            p_i = n * ppb + pg
            kp = k_pages[p_i][0]
            ksum = ksum + kp
            kbuf[:, p_i * PAGE_SIZE:(p_i + 1) * PAGE_SIZE] = kp.astype(BF16)
            vbuf[:, p_i * PAGE_SIZE:(p_i + 1) * PAGE_SIZE] = v_pages[p_i][0].astype(BF16)
        kmean = jnp.sum(ksum, axis=1, keepdims=True) * (1.0 / MOBA_BLOCK)
        km_t = jnp.where(km_lane == n, kmean, km_t)

    gate = jnp.dot(qbd, km_t, preferred_element_type=F32, precision=lax.Precision.HIGHEST)
    glane = lax.broadcasted_iota(jnp.int32, (rows, LANES), 1)
    sel = _topk_mask(gate, glane < nbp, glane.astype(F32), min(MOBA_TOPK, nbp), 1)
    bias = jnp.where(sel, 0.0, NEG_BIG)

    zrows = jnp.zeros((ln, da), F32)
    kn = jnp.concatenate([kn_ref[0], zrows], axis=0).astype(BF16)
    vn = jnp.concatenate([vn_ref[0], zrows], axis=0).astype(BF16)
    s_own = _dot_t(qbd_bf, kn)
    orow = lax.broadcasted_iota(jnp.int32, (rows, 2 * ln), 0)
    ocol = lax.broadcasted_iota(jnp.int32, (rows, 2 * ln), 1)
    s_own = jnp.where(ocol <= orow % ln, s_own, NEG_BIG)
    m = jnp.max(s_own, axis=-1, keepdims=True)

    s_all = _dot(qbd_bf, kbuf[...])
    s_past = []
    for n in range(nbp):
        s = s_all[:, n * MOBA_BLOCK:(n + 1) * MOBA_BLOCK] + bias[:, n:n + 1]
        s_past.append(s)
        m = jnp.maximum(m, jnp.max(s, axis=-1, keepdims=True))

    p_own = jnp.exp(s_own - m)
    lsum = jnp.sum(p_own, axis=-1, keepdims=True)
    p_past = []
    for n in range(nbp):
        p = jnp.exp(s_past[n] - m)
        lsum = lsum + jnp.sum(p, axis=-1, keepdims=True)
        p_past.append(p.astype(BF16))
    acc = _dot(p_own.astype(BF16), vn) + _dot_t(jnp.concatenate(p_past, axis=1), vbuf[...])

    o = jnp.where(own_head, acc / lsum, 0.0)
    out = o[0:ln]
    for hh in range(1, nh):
        out = out + o[hh * ln:(hh + 1) * ln]
    o_ref[0] = out


def _attn_sample(l, q3, k3, v3, ck_t, cv_t, pt_flat, n_pool, n_pages):
    bd, ln, da = q3.shape
    nh = da // HEAD_DIM
    kern = functools.partial(_attn_sample_kernel, n_pages=n_pages, nh=nh, ln=ln)

    def page_spec(p_i):
        return pl.BlockSpec((1, da, PAGE_SIZE), lambda i, pt: (l * n_pool + pt[i * n_pages + p_i], 0, 0))

    tok = pl.BlockSpec((1, ln, da), lambda i, pt: (i, 0, 0))
    grid_spec = pltpu.PrefetchScalarGridSpec(
        num_scalar_prefetch=1,
        grid=(bd,),
        in_specs=[tok, tok, tok] + [page_spec(p_i) for p_i in range(n_pages)] * 2,
        out_specs=tok,
        scratch_shapes=[pltpu.VMEM((da, n_pages * PAGE_SIZE), BF16),
                        pltpu.VMEM((da, n_pages * PAGE_SIZE), BF16)],
    )
    return pl.pallas_call(
        kern,
        grid_spec=grid_spec,
        out_shape=jax.ShapeDtypeStruct((bd, ln, da), F32),
        compiler_params=_cparams("arbitrary"),
        name=f"attn_sample_{l}",
    )(pt_flat, q3, k3, v3, *([ck_t] * n_pages), *([cv_t] * n_pages))


def _outmlp_kernel(x_ref, yc_ref, ya_ref, g1_ref, sc2_ref, sh2_ref, g2_ref, oga_ref, wout_ref,
                   ln1g_ref, ln1b_ref, w1_ref, b1_ref, w2_ref, b2_ref, ln2g_ref, ln2b_ref, o_ref,
                   *, alpha, ff_chunk):
    x = x_ref[...]
    blk = x.shape
    rows, d = blk[0] * blk[1], blk[2]
    dc = yc_ref.shape[-1]
    dff = w1_ref.shape[-1]

    yan = _rms_scale(ya_ref[...], oga_ref[0]).astype(BF16)
    mix = _dot(yc_ref[...], wout_ref[0, 0:dc, :]) + _dot(yan, wout_ref[0, dc:, :])
    x1 = _layer_norm(alpha * x + (1.0 + g1_ref[0]) * mix.reshape(blk), ln1g_ref[0], ln1b_ref[0])

    h2 = (x1 * (1.0 + sc2_ref[0]) + sh2_ref[0]).reshape(rows, d).astype(BF16)
    f = jnp.broadcast_to(b2_ref[0], (rows, d))
    for c in range(dff // ff_chunk):
        cs = slice(c * ff_chunk, (c + 1) * ff_chunk)
        hid = jnp.maximum(_dot(h2, w1_ref[0, :, cs]) + b1_ref[0, :, cs], 0.0)
        f = f + _dot((hid * hid).astype(BF16), w2_ref[0, cs, :])
    o_ref[...] = _layer_norm(alpha * x1 + (1.0 + g2_ref[0]) * f.reshape(blk), ln2g_ref[0], ln2b_ref[0])


def _outmlp(l, x3, yc2, ya2, mod, mod_row0, per_row_mod, oga3, w_out_bf, ln1g3, ln1b3, w1_bf, b13, w2_bf,
            b23, ln2g3, ln2b3, alpha, grp, tag):
    a, r, d = x3.shape
    dc, da = yc2.shape[-1], ya2.shape[-1]
    dff = w1_bf.shape[-1]
    if per_row_mod:
        blk = (grp, r, d)
        grid = (a // grp,)
        xmap = lambda i: (i, 0, 0)
        rmap = lambda i: (i, 0)
        brows = grp * r
        mblk = (1, grp, 1, d)

        def mod_spec(comp):
            return pl.BlockSpec(mblk, lambda i: (l * N_MOD + comp, i, 0, 0))
        sem = ("parallel",)
    else:
        nt = r // grp
        blk = (1, grp, d)
        grid = (a, nt)
        xmap = lambda bi, i: (bi, i, 0)
        rmap = lambda bi, i: (bi * nt + i, 0)
        brows = grp
        mblk = (1, 1, 1, d)

        def mod_spec(comp):
            return pl.BlockSpec(mblk, lambda bi, i: (l * N_MOD + comp, mod_row0 + bi, 0, 0))
        sem = ("parallel", "parallel")

    nargs = len(grid)

    def const(shape):
        zeros = (0,) * (len(shape) - 1)
        if nargs == 1:
            return pl.BlockSpec(shape, lambda i: (l,) + zeros, pipeline_mode=pl.Buffered(1))
        return pl.BlockSpec(shape, lambda bi, i: (l,) + zeros, pipeline_mode=pl.Buffered(1))

    kern = functools.partial(_outmlp_kernel, alpha=alpha, ff_chunk=min(dff, 1024))
    return pl.pallas_call(
        kern,
        grid=grid,
        in_specs=[pl.BlockSpec(blk, xmap),
                  pl.BlockSpec((brows, dc), rmap),
                  pl.BlockSpec((brows, da), rmap),
                  mod_spec(G1), mod_spec(SC2), mod_spec(SH2), mod_spec(G2),
                  const((1, 1, da)), const((1, dc + da, d)),
                  const((1, 1, d)), const((1, 1, d)),
                  const((1, d, dff)), const((1, 1, dff)), const((1, dff, d)), const((1, 1, d)),
                  const((1, 1, d)), const((1, 1, d))],
        out_specs=pl.BlockSpec(blk, xmap),
        out_shape=jax.ShapeDtypeStruct((a, r, d), F32),
        compiler_params=_cparams(*sem),
        name=f"outmlp_{tag}_{l}",
    )(x3, yc2, ya2, mod, mod, mod, mod, oga3, w_out_bf, ln1g3, ln1b3, w1_bf, b13, w2_bf, b23, ln2g3, ln2b3)


def kernel(x_prompt, x_sample, cache_k, cache_v, state_conv, page_table, c_prompt, c_sample, ln0_g, ln0_b, w_ada, b_ada, w_in, w_dw, b_dw, conv_ln_g, conv_ln_b, out_g_conv, out_g_attn, w_out, ln1_g, ln1_b, w1, b1, w2, b2, ln2_g, ln2_b):
    b, s, d = x_prompt.shape
    bd, ln, _ = x_sample.shape
    depth = w_in.shape[0]
    dc = w_dw.shape[-1]
    da = out_g_attn.shape[-1]
    nh = da // HEAD_DIM
    n_pool = cache_k.shape[1]
    n_pages = page_table.shape[1]
    alpha = (2 * depth) ** 0.25
    assert (n_pages * PAGE_SIZE) % MOBA_BLOCK == 0 and ln <= MOBA_BLOCK and ln == SUBLANES
    assert s % MOBA_BLOCK == 0 and s // MOBA_BLOCK <= HEAD_DIM and da % LANES == 0
    tm = min(PROMPT_ROWS, s)
    tb = min(SAMPLE_SEQS, bd)

    w_ada_bf, w_in_bf, w_out_bf = w_ada.astype(BF16), w_in.astype(BF16), w_out.astype(BF16)
    w1_bf, w2_bf = w1.astype(BF16), w2.astype(BF16)

    def vec3(a):
        return a.reshape(depth, 1, a.shape[-1])

    b_dw3, clg3, clb3, ogc3, oga3 = vec3(b_dw), vec3(conv_ln_g), vec3(conv_ln_b), vec3(out_g_conv), vec3(out_g_attn)
    ln1g3, ln1b3, ln2g3, ln2b3, b13, b23 = vec3(ln1_g), vec3(ln1_b), vec3(ln2_g), vec3(ln2_b), vec3(b1), vec3(b2)

    n_rows = bd + SUBLANES
    assert b <= SUBLANES and bd % SUBLANES == 0
    c_all = jnp.concatenate([c_sample, c_prompt, jnp.zeros((SUBLANES - b, d), F32)], axis=0)
    mod = _ada(c_all, w_ada_bf, b_ada)
    mod_row0 = n_rows - SUBLANES

    xp = _input_ln(x_prompt.reshape(b * s, d), ln0_g, ln0_b, tm).reshape(b, s, d)
    xs = _input_ln(x_sample.reshape(bd * ln, d), ln0_g, ln0_b, min(bd * ln, 512)).reshape(bd, ln, d)

    state_pad = jnp.pad(state_conv, ((0, 0), (0, 0), (CONV_HALO - (CONV_WIDTH - 1), 0), (0, 0)))
    ck2 = cache_k.transpose(0, 1, 3, 4, 2).reshape(depth * n_pool, da, PAGE_SIZE)
    cv2 = cache_v.transpose(0, 1, 3, 4, 2).reshape(depth * n_pool, da, PAGE_SIZE)
    pt_flat = page_table.reshape(-1).astype(jnp.int32)

    kp, vp, cp, ksm, vsm, csm = [], [], [], [], [], []
    for l in range(depth):
        yc, q2, k2, v2, kb2, vb2, km3, cn = _inproj_prompt(l, xp, mod, w_in_bf, w_dw, b_dw3, clg3, clb3, ogc3, tm)
        ya = _attn_prompt(l, q2, kb2, vb2, km3, b, s)
        xp = _outmlp(l, xp, yc, ya, mod, mod_row0, False, oga3, w_out_bf, ln1g3, ln1b3, w1_bf, b13, w2_bf, b23,
                     ln2g3, ln2b3, alpha, tm, "prompt")
        kp.append(k2.reshape(b, s, nh, HEAD_DIM))
        vp.append(v2.reshape(b, s, nh, HEAD_DIM))
        cp.append(cn[:, CONV_HALO - (CONV_WIDTH - 1):])

        ycs, qs, ks, vs, cns = _inproj_sample(l, xs, mod, state_pad, w_in_bf, w_dw, b_dw3, clg3, clb3, ogc3, tb)
        yas = _attn_sample(l, qs.reshape(bd, ln, da), ks.reshape(bd, ln, da), vs.reshape(bd, ln, da),
                           ck2, cv2, pt_flat, n_pool, n_pages)
        xs = _outmlp(l, xs, ycs, yas.reshape(bd * ln, da), mod, 0, True, oga3, w_out_bf, ln1g3, ln1b3, w1_bf,
                     b13, w2_bf, b23, ln2g3, ln2b3, alpha, tb, "sample")
        ksm.append(ks.reshape(bd, ln, nh, HEAD_DIM))
        vsm.append(vs.reshape(bd, ln, nh, HEAD_DIM))
        csm.append(cns[:, CONV_HALO - (CONV_WIDTH - 1):])

    return (xp, xs, jnp.stack(kp), jnp.stack(vp), jnp.stack(cp), jnp.stack(ksm), jnp.stack(vsm), jnp.stack(csm))
```

```python
import functools

import jax
import jax.numpy as jnp
from jax import lax
from jax.experimental import pallas as pl
from jax.experimental.pallas import tpu as pltpu

F32 = jnp.float32
BF16 = jnp.bfloat16

LN_EPS = 1e-5
HEAD_DIM = 64
MOBA_BLOCK = 256
MOBA_TOPK = 3
CONV_WIDTH = 31
PAGE_SIZE = 128
N_MOD = 6
SH1, SC1, G1, SH2, SC2, G2 = range(N_MOD)

SUBLANES = 8
LANES = 128
CONV_HALO = 32
CONV_ROWS = 64
NEG_BIG = -1e30
LOG2_E = 1.4426950408889634
VMEM_LIMIT = 56 * 1024 * 1024

PROMPT_ROWS = 512
SAMPLE_SEQS = 32
KV_TILE_BLOCKS = 4
SAMPLE_ATTN_SEQS = 2


def _cparams(*sem):
    return pltpu.CompilerParams(dimension_semantics=sem, vmem_limit_bytes=VMEM_LIMIT)


def _layer_norm(x, g, b):
    mu = jnp.mean(x, axis=-1, keepdims=True)
    xc = x - mu
    var = jnp.mean(xc * xc, axis=-1, keepdims=True)
    return xc * lax.rsqrt(var + LN_EPS) * g + b


def _rms_scale(x, g):
    return x * lax.rsqrt(jnp.mean(x * x, axis=-1, keepdims=True) + LN_EPS) * g


def _dot(a, b):
    return jnp.dot(a, b, preferred_element_type=F32)


def _dot_t(a, b, precision=None):
    return lax.dot_general(a, b, (((1,), (1,)), ((), ())), preferred_element_type=F32, precision=precision)


def _ln_kernel(x_ref, g_ref, b_ref, o_ref):
    o_ref[...] = _layer_norm(x_ref[...], g_ref[...], b_ref[...])


def _input_ln(x2, g, b, rows):
    n, d = x2.shape
    return pl.pallas_call(
        _ln_kernel,
        grid=(n // rows,),
        in_specs=[pl.BlockSpec((rows, d), lambda i: (i, 0)),
                  pl.BlockSpec((1, d), lambda i: (0, 0)),
                  pl.BlockSpec((1, d), lambda i: (0, 0))],
        out_specs=pl.BlockSpec((rows, d), lambda i: (i, 0)),
        out_shape=jax.ShapeDtypeStruct((n, d), F32),
        compiler_params=_cparams("parallel"),
        name="input_ln",
    )(x2, g.reshape(1, d), b.reshape(1, d))


def _ada_kernel(c_ref, w_ref, b_ref, o_ref):
    c = c_ref[...]
    h = (c * jax.nn.sigmoid(c)).astype(BF16)
    o_ref[0] = _dot(h, w_ref[0]) + b_ref[0]


def _ada(c_all, w_ada_bf, b_ada):
    depth, d, _ = w_ada_bf.shape
    r = c_all.shape[0]
    out = pl.pallas_call(
        _ada_kernel,
        grid=(depth, N_MOD),
        in_specs=[pl.BlockSpec((r, d), lambda l, j: (0, 0)),
                  pl.BlockSpec((1, d, d), lambda l, j: (l, 0, j)),
                  pl.BlockSpec((1, 1, d), lambda l, j: (l * N_MOD + j, 0, 0))],
        out_specs=pl.BlockSpec((1, r, d), lambda l, j: (l * N_MOD + j, 0, 0)),
        out_shape=jax.ShapeDtypeStruct((depth * N_MOD, r, d), F32),
        compiler_params=_cparams("parallel", "parallel"),
        name="ada_mod",
    )(c_all, w_ada_bf, b_ada.reshape(depth * N_MOD, 1, d))
    return out.reshape(depth * N_MOD, r, 1, d)


def _conv_post(y, clg, clb, ogc):
    yn = _layer_norm(y, clg, clb)
    ys = yn * jax.nn.sigmoid(yn)
    return _rms_scale(ys, ogc).astype(BF16)


def _inproj_prompt_kernel(x_ref, sc_ref, sh_ref, w_ref, wt_ref, wdw_ref, bdw_ref, clg_ref, clb_ref, ogc_ref,
                          yc_ref, qt_ref, kt_ref, vt_ref, kb_ref, vbt_ref, km_ref, cn_ref, ubuf, ushift,
                          *, tm, dc, da):
    i = pl.program_id(1)

    @pl.when(i == 0)
    def _():
        ubuf[0:CONV_HALO, :] = jnp.zeros((CONV_HALO, dc), F32)

    x = x_ref[0]
    h = (x * (1.0 + sc_ref[0, 0]) + sh_ref[0, 0]).astype(BF16)

    zc = _dot(h, w_ref[0, :, 0:2 * dc])
    ubuf[CONV_HALO:CONV_HALO + tm, :] = zc[:, :dc] * jax.nn.sigmoid(zc[:, dc:])

    first = CONV_HALO - (CONV_WIDTH - 1)
    span = tm + CONV_HALO - SUBLANES
    for r in range(1, SUBLANES):
        ushift[r - 1] = ubuf[r:r + span, :]
    for c in range(tm // CONV_ROWS):
        acc = jnp.broadcast_to(bdw_ref[0], (CONV_ROWS, dc))
        for j in range(CONV_WIDTH):
            a, r = divmod(first + j, SUBLANES)
            r0 = c * CONV_ROWS + a * SUBLANES
            src = ubuf[r0:r0 + CONV_ROWS, :] if r == 0 else ushift[r - 1, r0:r0 + CONV_ROWS, :]
            acc = acc + wdw_ref[0, j:j + 1, :] * src
        yc_ref[c * CONV_ROWS:(c + 1) * CONV_ROWS, :] = _conv_post(acc, clg_ref[0], clb_ref[0], ogc_ref[0])

    @pl.when(i == pl.num_programs(1) - 1)
    def _():
        cn_ref[0] = ubuf[tm:tm + CONV_HALO, :]

    ubuf[0:CONV_HALO, :] = ubuf[tm:tm + CONV_HALO, :]

    qt_ref[...] = _dot_t(wt_ref[0, 0:da, :], h) * (HEAD_DIM ** -0.5 * LOG2_E)
    kt_ref[...] = _dot_t(wt_ref[0, da:2 * da, :], h)
    zvt = _dot_t(wt_ref[0, 2 * da:3 * da, :], h)
    vt_ref[...] = zvt
    vbt_ref[...] = zvt.astype(BF16)
    c0 = 2 * dc
    zk = _dot(h, w_ref[0, :, c0 + da:c0 + 2 * da])
    kb_ref[...] = zk.astype(BF16)
    for r in range(tm // MOBA_BLOCK):
        km_ref[0, r:r + 1, :] = jnp.mean(zk[r * MOBA_BLOCK:(r + 1) * MOBA_BLOCK], axis=0, keepdims=True)


def _inproj_prompt(l, x3, mod, w_in_bf, w_qkv_t, w_dw, b_dw3, clg3, clb3, ogc3, tm):
    b, s, d = x3.shape
    dc = w_dw.shape[-1]
    da = (w_in_bf.shape[-1] - 2 * dc) // 3
    r = mod.shape[1]
    nt = s // tm
    nbt = tm // MOBA_BLOCK
    n = b * s
    prow = r - SUBLANES
    kern = functools.partial(_inproj_prompt_kernel, tm=tm, dc=dc, da=da)

    def vec(a):
        return pl.BlockSpec((1, 1, a.shape[-1]), lambda bi, i: (l, 0, 0))

    def rows(width):
        return pl.BlockSpec((tm, width), lambda bi, i: (bi * nt + i, 0))

    cols = pl.BlockSpec((da, tm), lambda bi, i: (bi, i))

    outs = pl.pallas_call(
        kern,
        grid=(b, nt),
        in_specs=[pl.BlockSpec((1, tm, d), lambda bi, i: (bi, i, 0)),
                  pl.BlockSpec((1, 1, 1, d), lambda bi, i: (l * N_MOD + SC1, prow + bi, 0, 0)),
                  pl.BlockSpec((1, 1, 1, d), lambda bi, i: (l * N_MOD + SH1, prow + bi, 0, 0)),
                  pl.BlockSpec((1, d, w_in_bf.shape[-1]), lambda bi, i: (l, 0, 0)),
                  pl.BlockSpec((1, 3 * da, d), lambda bi, i: (l, 0, 0)),
                  pl.BlockSpec((1, CONV_WIDTH, dc), lambda bi, i: (l, 0, 0)),
                  vec(b_dw3), vec(clg3), vec(clb3), vec(ogc3)],
        out_specs=[rows(dc), cols, cols, cols, rows(da), cols,
                   pl.BlockSpec((1, nbt, da), lambda bi, i: (bi * nt + i, 0, 0)),
                   pl.BlockSpec((1, CONV_HALO, dc), lambda bi, i: (bi, 0, 0))],
        out_shape=[jax.ShapeDtypeStruct((n, dc), BF16),
                   jax.ShapeDtypeStruct((b * da, s), F32),
                   jax.ShapeDtypeStruct((b * da, s), F32),
                   jax.ShapeDtypeStruct((b * da, s), F32),
                   jax.ShapeDtypeStruct((n, da), BF16),
                   jax.ShapeDtypeStruct((b * da, s), BF16),
                   jax.ShapeDtypeStruct((b * nt, nbt, da), F32),
                   jax.ShapeDtypeStruct((b, CONV_HALO, dc), F32)],
        scratch_shapes=[pltpu.VMEM((CONV_HALO + tm, dc), F32),
                        pltpu.VMEM((SUBLANES - 1, tm + CONV_HALO - SUBLANES, dc), F32)],
        compiler_params=_cparams("parallel", "arbitrary"),
        name=f"inproj_prompt_{l}",
    )(x3, mod, mod, w_in_bf, w_qkv_t, w_dw, b_dw3, clg3, clb3, ogc3)
    return outs


def _inproj_sample_kernel(x_ref, sc_ref, sh_ref, st_ref, w_ref, wdw_ref, bdw_ref, clg_ref, clb_ref, ogc_ref,
                          yc_ref, q_ref, k_ref, v_ref, cn_ref, uext, *, tb, ln, dc, da):
    d = x_ref.shape[-1]
    h = (x_ref[...] * (1.0 + sc_ref[0]) + sh_ref[0]).reshape(tb * ln, d).astype(BF16)

    zc = _dot(h, w_ref[0, :, 0:2 * dc])
    u = zc[:, :dc] * jax.nn.sigmoid(zc[:, dc:])
    uext[:, 0:CONV_HALO, :] = st_ref[...]
    uext[:, CONV_HALO:CONV_HALO + ln, :] = u.reshape(tb, ln, dc)

    first = CONV_HALO - (CONV_WIDTH - 1)
    acc = jnp.broadcast_to(bdw_ref[0], (tb, ln, dc))
    for j in range(CONV_WIDTH):
        acc = acc + wdw_ref[0, j:j + 1, :] * uext[:, first + j:first + j + ln, :]
    yc_ref[...] = _conv_post(acc.reshape(tb * ln, dc), clg_ref[0], clb_ref[0], ogc_ref[0])
    cn_ref[...] = uext[:, ln:ln + CONV_HALO, :]

    c0 = 2 * dc
    q_ref[...] = _dot(h, w_ref[0, :, c0:c0 + da]) * (HEAD_DIM ** -0.5)
    k_ref[...] = _dot(h, w_ref[0, :, c0 + da:c0 + 2 * da])
    v_ref[...] = _dot(h, w_ref[0, :, c0 + 2 * da:c0 + 3 * da])


def _inproj_sample(l, x3, mod, state_pad, w_in_bf, w_dw, b_dw3, clg3, clb3, ogc3, tb):
    bd, ln, d = x3.shape
    dc = w_dw.shape[-1]
    da = (w_in_bf.shape[-1] - 2 * dc) // 3
    n = bd * ln
    kern = functools.partial(_inproj_sample_kernel, tb=tb, ln=ln, dc=dc, da=da)

    def vec(a):
        return pl.BlockSpec((1, 1, a.shape[-1]), lambda i: (l, 0, 0))

    def rows(width):
        return pl.BlockSpec((tb * ln, width), lambda i: (i, 0))

    return pl.pallas_call(
        kern,
        grid=(bd // tb,),
        in_specs=[pl.BlockSpec((tb, ln, d), lambda i: (i, 0, 0)),
                  pl.BlockSpec((1, tb, 1, d), lambda i: (l * N_MOD + SC1, i, 0, 0)),
                  pl.BlockSpec((1, tb, 1, d), lambda i: (l * N_MOD + SH1, i, 0, 0)),
                  pl.BlockSpec((None, tb, CONV_HALO, dc), lambda i: (l, i, 0, 0)),
                  pl.BlockSpec((1, d, w_in_bf.shape[-1]), lambda i: (l, 0, 0)),
                  pl.BlockSpec((1, CONV_WIDTH, dc), lambda i: (l, 0, 0)),
                  vec(b_dw3), vec(clg3), vec(clb3), vec(ogc3)],
        out_specs=[rows(dc), rows(da), rows(da), rows(da),
                   pl.BlockSpec((tb, CONV_HALO, dc), lambda i: (i, 0, 0))],
        out_shape=[jax.ShapeDtypeStruct((n, dc), BF16),
                   jax.ShapeDtypeStruct((n, da), F32),
                   jax.ShapeDtypeStruct((n, da), F32),
                   jax.ShapeDtypeStruct((n, da), F32),
                   jax.ShapeDtypeStruct((bd, CONV_HALO, dc), F32)],
        scratch_shapes=[pltpu.VMEM((tb, CONV_HALO + ln, dc), F32)],
        compiler_params=_cparams("parallel"),
        name=f"inproj_sample_{l}",
    )(x3, mod, mod, state_pad, w_in_bf, w_dw, b_dw3, clg3, clb3, ogc3)


def _topk_mask(gate, valid, pos, k, axis):
    g = jnp.where(valid, gate, -jnp.inf)
    sel = jnp.zeros(gate.shape, jnp.bool_)
    for _ in range(k):
        mx = jnp.max(g, axis=axis, keepdims=True)
        idx = jnp.min(jnp.where(g == mx, pos, float(gate.shape[axis])), axis=axis, keepdims=True)
        pick = (pos == idx) & (mx > -jnp.inf)
        sel = sel | pick
        g = jnp.where(pick, -jnp.inf, g)
    return sel


def _attn_prompt_kernel(qt_ref, kb_ref, vbt_ref, km_ref, o_ref, kaug, qaug, s_a, s_b, m_sc, l_sc, acc_sc,
                        *, nb, kvb):
    j = pl.program_id(2)
    tq = MOBA_BLOCK
    kc = kvb * MOBA_BLOCK
    lane = lax.broadcasted_iota(jnp.int32, (tq, LANES), 1)
    head_lanes = (lane < HEAD_DIM, lane >= HEAD_DIM)
    pen_base = (HEAD_DIM, 0)

    @pl.when(j == 0)
    def _():
        for n in range(nb):
            kblk = kb_ref[n * tq:(n + 1) * tq, :]
            for hh in range(2):
                onehot = jnp.where(lane == pen_base[hh] + n, 1.0, 0.0).astype(BF16)
                kaug[hh, n * tq:(n + 1) * tq, :] = jnp.where(head_lanes[hh], kblk, onehot)

    qt = qt_ref[...]
    km = km_ref[0]
    lane_nb = lax.broadcasted_iota(jnp.int32, (nb, LANES), 1)
    blk_i = lax.broadcasted_iota(jnp.int32, (nb, tq), 0)
    blk_f = blk_i.astype(F32)
    zfill = jnp.zeros((HEAD_DIM - nb, tq), F32)
    for hh in range(2):
        hl_nb = (lane_nb < HEAD_DIM) if hh == 0 else (lane_nb >= HEAD_DIM)
        kmh = jnp.where(hl_nb, km, 0.0)
        gate_t = jnp.dot(kmh, qt, preferred_element_type=F32, precision=lax.Precision.HIGHEST)
        sel = _topk_mask(gate_t, blk_i < j, blk_f, min(MOBA_TOPK, nb), 0)
        pen_t = jnp.where(sel | (blk_i == j), 0.0, NEG_BIG)
        if hh == 0:
            rows_t = [qt[0:HEAD_DIM], pen_t, zfill]
        else:
            rows_t = [pen_t, zfill, qt[HEAD_DIM:2 * HEAD_DIM]]
        qaug[hh] = jnp.concatenate(rows_t, axis=0).astype(BF16)

    m_sc[...] = jnp.full(m_sc.shape, -jnp.inf, F32)
    l_sc[...] = jnp.zeros(l_sc.shape, F32)
    acc_sc[...] = jnp.zeros(acc_sc.shape, F32)

    def scores(c, buf, causal):
        k0 = pl.multiple_of(c * kc, kc)
        for hh in range(2):
            for sb in range(kvb):
                ks = k0 + sb * tq
                buf[hh, sb * tq:(sb + 1) * tq, :] = _dot(kaug[hh, pl.ds(ks, tq), :], qaug[hh])
        if causal:
            own = pl.multiple_of((j % kvb) * tq, tq)
            key_i = lax.broadcasted_iota(jnp.int32, (tq, tq), 0)
            qry_i = lax.broadcasted_iota(jnp.int32, (tq, tq), 1)
            for hh in range(2):
                buf[hh, pl.ds(own, tq), :] = jnp.where(key_i <= qry_i, buf[hh, pl.ds(own, tq), :], NEG_BIG)

    def softmax_pv(c, buf):
        k0 = pl.multiple_of(c * kc, kc)
        m_hd = []
        for hh in range(2):
            m_cur = jnp.max(buf[hh, 0:tq, :], axis=0, keepdims=True)
            for sb in range(1, kvb):
                m_cur = jnp.maximum(m_cur, jnp.max(buf[hh, sb * tq:(sb + 1) * tq, :], axis=0, keepdims=True))
            m_hd.append(jnp.maximum(m_sc[hh], m_cur))
        for hh in range(2):
            m_row = m_hd[hh][0:1]
            alpha = jnp.exp2(m_sc[hh] - m_hd[hh])
            p_parts = [jnp.exp2(buf[hh, sb * tq:(sb + 1) * tq, :] - m_row) for sb in range(kvb)]
            psum = jnp.sum(p_parts[0], axis=0, keepdims=True)
            for sb in range(1, kvb):
                psum = psum + jnp.sum(p_parts[sb], axis=0, keepdims=True)
            l_sc[hh] = alpha * l_sc[hh] + psum
            v_t = vbt_ref[hh * HEAD_DIM:(hh + 1) * HEAD_DIM, pl.ds(k0, kc)]
            p_all = jnp.concatenate([p.astype(BF16) for p in p_parts], axis=0)
            acc_sc[hh] = alpha[0:1] * acc_sc[hh] + _dot(v_t, p_all)
            m_sc[hh] = m_hd[hh]

    c_own = j // kvb
    bufs = (s_a, s_b)
    scores(c_own, bufs[0], True)
    for i in range(nb // kvb):
        @pl.when(i <= c_own)
        def _():
            scores(jnp.maximum(c_own - i - 1, 0), bufs[(i + 1) % 2], False)
            softmax_pv(c_own - i, bufs[i % 2])


    o_t = jnp.concatenate([acc_sc[0] / l_sc[0, 0:1], acc_sc[1] / l_sc[1, 0:1]], axis=0)
    o_ref[...] = o_t.T


def _attn_prompt(l, qt2, kb2, vbt2, km3, b, s):
    n, da = kb2.shape
    nb = s // MOBA_BLOCK
    hp = da // LANES
    tq = MOBA_BLOCK
    kvb = max(c for c in range(1, KV_TILE_BLOCKS + 1) if nb % c == 0)
    km3 = km3.reshape(b, nb, da)
    kern = functools.partial(_attn_prompt_kernel, nb=nb, kvb=kvb)
    return pl.pallas_call(
        kern,
        grid=(b, hp, nb),
        in_specs=[pl.BlockSpec((LANES, tq), lambda bi, p, j: (bi * hp + p, j)),
                  pl.BlockSpec((s, LANES), lambda bi, p, j: (bi, p)),
                  pl.BlockSpec((LANES, s), lambda bi, p, j: (bi * hp + p, 0)),
                  pl.BlockSpec((1, nb, LANES), lambda bi, p, j: (bi, 0, p))],
        out_specs=pl.BlockSpec((tq, LANES), lambda bi, p, j: (bi * nb + j, p)),
        out_shape=jax.ShapeDtypeStruct((n, da), F32),
        scratch_shapes=[pltpu.VMEM((2, s, LANES), BF16),
                        pltpu.VMEM((2, LANES, tq), BF16),
                        pltpu.VMEM((2, kvb * tq, tq), F32),
                        pltpu.VMEM((2, kvb * tq, tq), F32),
                        pltpu.VMEM((2, SUBLANES, tq), F32),
                        pltpu.VMEM((2, SUBLANES, tq), F32),
                        pltpu.VMEM((2, HEAD_DIM, tq), F32)],
        compiler_params=_cparams("parallel", "parallel", "arbitrary"),
        name=f"attn_prompt_{l}",
    )(qt2, kb2, vbt2, km3)


def _attn_sample_kernel(pt_ref, q_ref, kn_ref, vn_ref, *refs, n_pages, nh, ln, spq):
    del pt_ref
    o_ref, kbuf, vbuf = refs[2 * spq * n_pages:]
    for sq in range(spq):
        k_pages = refs[sq * n_pages:(sq + 1) * n_pages]
        v_pages = refs[(spq + sq) * n_pages:(spq + sq + 1) * n_pages]
        o_ref[sq] = _attn_sample_one(q_ref[sq], kn_ref[sq], vn_ref[sq], k_pages, v_pages, kbuf.at[sq], vbuf.at[sq],
                                     nh=nh, ln=ln)


def _attn_sample_one(q, kn_new, vn_new, k_pages, v_pages, kbuf, vbuf, *, nh, ln):
    n_pages = len(k_pages)
    da = nh * HEAD_DIM
    rows = nh * ln
    ppb = MOBA_BLOCK // PAGE_SIZE
    nbp = n_pages // ppb

    qt = jnp.concatenate([q] * nh, axis=0)
    row = lax.broadcasted_iota(jnp.int32, (rows, da), 0)
    lane = lax.broadcasted_iota(jnp.int32, (rows, da), 1)
    own_head = (lane // HEAD_DIM) == (row // ln)
    qbd = jnp.where(own_head, qt, 0.0)
    qbd_bf = qbd.astype(BF16)

    km_lane = lax.broadcasted_iota(jnp.int32, (da, LANES), 1)
    km_t = jnp.zeros((da, LANES), F32)
    for n in range(nbp):
        ksum = jnp.zeros((da, PAGE_SIZE), F32)
        for pg in range(ppb):
            p_i = n * ppb + pg
            kp = k_pages[p_i][0]
            ksum = ksum + kp
            kbuf[:, p_i * PAGE_SIZE:(p_i + 1) * PAGE_SIZE] = kp.astype(BF16)
            vbuf[:, p_i * PAGE_SIZE:(p_i + 1) * PAGE_SIZE] = v_pages[p_i][0].astype(BF16)
        kmean = jnp.sum(ksum, axis=1, keepdims=True) * (1.0 / MOBA_BLOCK)
        km_t = jnp.where(km_lane == n, kmean, km_t)

    gate = jnp.dot(qbd, km_t, preferred_element_type=F32, precision=lax.Precision.HIGHEST)
    glane = lax.broadcasted_iota(jnp.int32, (rows, LANES), 1)
    sel = _topk_mask(gate, glane < nbp, glane.astype(F32), min(MOBA_TOPK, nbp), 1)
    bias = jnp.where(sel, 0.0, NEG_BIG)

    zrows = jnp.zeros((ln, da), F32)
    kn = jnp.concatenate([kn_new, zrows], axis=0).astype(BF16)
    vn = jnp.concatenate([vn_new, zrows], axis=0).astype(BF16)
    s_own = _dot_t(qbd_bf, kn)
    orow = lax.broadcasted_iota(jnp.int32, (rows, 2 * ln), 0)
    ocol = lax.broadcasted_iota(jnp.int32, (rows, 2 * ln), 1)
    s_own = jnp.where(ocol <= orow % ln, s_own, NEG_BIG)
    m = jnp.max(s_own, axis=-1, keepdims=True)

    s_all = _dot(qbd_bf, kbuf[...])
    s_past = []
    for n in range(nbp):
        s = s_all[:, n * MOBA_BLOCK:(n + 1) * MOBA_BLOCK] + bias[:, n:n + 1]
        s_past.append(s)
        m = jnp.maximum(m, jnp.max(s, axis=-1, keepdims=True))

    p_own = jnp.exp(s_own - m)
    lsum = jnp.sum(p_own, axis=-1, keepdims=True)
    p_past = []
    for n in range(nbp):
        p = jnp.exp(s_past[n] - m)
        lsum = lsum + jnp.sum(p, axis=-1, keepdims=True)
        p_past.append(p.astype(BF16))
    acc = _dot(p_own.astype(BF16), vn) + _dot_t(jnp.concatenate(p_past, axis=1), vbuf[...])

    o = jnp.where(own_head, acc / lsum, 0.0)
    out = o[0:ln]
    for hh in range(1, nh):
        out = out + o[hh * ln:(hh + 1) * ln]
    return out


def _attn_sample(l, q3, k3, v3, ck_t, cv_t, pt_flat, n_pool, n_pages):
    bd, ln, da = q3.shape
    nh = da // HEAD_DIM
    spq = SAMPLE_ATTN_SEQS if bd % SAMPLE_ATTN_SEQS == 0 else 1
    kern = functools.partial(_attn_sample_kernel, n_pages=n_pages, nh=nh, ln=ln, spq=spq)

    def page_spec(sq, p_i):
        return pl.BlockSpec((1, da, PAGE_SIZE),
                            lambda i, pt: (l * n_pool + pt[(i * spq + sq) * n_pages + p_i], 0, 0))

    pages = [page_spec(sq, p_i) for sq in range(spq) for p_i in range(n_pages)]
    tok = pl.BlockSpec((spq, ln, da), lambda i, pt: (i, 0, 0))
    grid_spec = pltpu.PrefetchScalarGridSpec(
        num_scalar_prefetch=1,
        grid=(bd // spq,),
        in_specs=[tok, tok, tok] + pages * 2,
        out_specs=tok,
        scratch_shapes=[pltpu.VMEM((spq, da, n_pages * PAGE_SIZE), BF16),
                        pltpu.VMEM((spq, da, n_pages * PAGE_SIZE), BF16)],
    )
    return pl.pallas_call(
        kern,
        grid_spec=grid_spec,
        out_shape=jax.ShapeDtypeStruct((bd, ln, da), F32),
        compiler_params=_cparams("arbitrary"),
        name=f"attn_sample_{l}",
    )(pt_flat, q3, k3, v3, *([ck_t] * (spq * n_pages)), *([cv_t] * (spq * n_pages)))


def _outmlp_kernel(x_ref, yc_ref, ya_ref, g1_ref, sc2_ref, sh2_ref, g2_ref, oga_ref, wout_ref,
                   ln1g_ref, ln1b_ref, w1_ref, b1_ref, w2_ref, b2_ref, ln2g_ref, ln2b_ref, o_ref,
                   *, alpha, ff_chunk):
    x = x_ref[...]
    blk = x.shape
    rows, d = blk[0] * blk[1], blk[2]
    dc = yc_ref.shape[-1]
    dff = w1_ref.shape[-1]

    yan = _rms_scale(ya_ref[...], oga_ref[0]).astype(BF16)
    mix = _dot(yc_ref[...], wout_ref[0, 0:dc, :]) + _dot(yan, wout_ref[0, dc:, :])
    x1 = _layer_norm(alpha * x + (1.0 + g1_ref[0]) * mix.reshape(blk), ln1g_ref[0], ln1b_ref[0])

    h2 = (x1 * (1.0 + sc2_ref[0]) + sh2_ref[0]).reshape(rows, d).astype(BF16)
    f = jnp.broadcast_to(b2_ref[0], (rows, d))
    for c in range(dff // ff_chunk):
        cs = slice(c * ff_chunk, (c + 1) * ff_chunk)
        hid = jnp.maximum(_dot(h2, w1_ref[0, :, cs]) + b1_ref[0, :, cs], 0.0)
        f = f + _dot((hid * hid).astype(BF16), w2_ref[0, cs, :])
    o_ref[...] = _layer_norm(alpha * x1 + (1.0 + g2_ref[0]) * f.reshape(blk), ln2g_ref[0], ln2b_ref[0])


def _outmlp(l, x3, yc2, ya2, mod, mod_row0, per_row_mod, oga3, w_out_bf, ln1g3, ln1b3, w1_bf, b13, w2_bf,
            b23, ln2g3, ln2b3, alpha, grp, tag):
    a, r, d = x3.shape
    dc, da = yc2.shape[-1], ya2.shape[-1]
    dff = w1_bf.shape[-1]
    if per_row_mod:
        blk = (grp, r, d)
        grid = (a // grp,)
        xmap = lambda i: (i, 0, 0)
        rmap = lambda i: (i, 0)
        brows = grp * r
        mblk = (1, grp, 1, d)

        def mod_spec(comp):
            return pl.BlockSpec(mblk, lambda i: (l * N_MOD + comp, i, 0, 0))
        sem = ("parallel",)
    else:
        nt = r // grp
        blk = (1, grp, d)
        grid = (a, nt)
        xmap = lambda bi, i: (bi, i, 0)
        rmap = lambda bi, i: (bi * nt + i, 0)
        brows = grp
        mblk = (1, 1, 1, d)

        def mod_spec(comp):
            return pl.BlockSpec(mblk, lambda bi, i: (l * N_MOD + comp, mod_row0 + bi, 0, 0))
        sem = ("parallel", "parallel")

    nargs = len(grid)

    def const(shape):
        zeros = (0,) * (len(shape) - 1)
        if nargs == 1:
            return pl.BlockSpec(shape, lambda i: (l,) + zeros, pipeline_mode=pl.Buffered(1))
        return pl.BlockSpec(shape, lambda bi, i: (l,) + zeros, pipeline_mode=pl.Buffered(1))

    kern = functools.partial(_outmlp_kernel, alpha=alpha, ff_chunk=min(dff, 1024))
    return pl.pallas_call(
        kern,
        grid=grid,
        in_specs=[pl.BlockSpec(blk, xmap),
                  pl.BlockSpec((brows, dc), rmap),
                  pl.BlockSpec((brows, da), rmap),
                  mod_spec(G1), mod_spec(SC2), mod_spec(SH2), mod_spec(G2),
                  const((1, 1, da)), const((1, dc + da, d)),
                  const((1, 1, d)), const((1, 1, d)),
                  const((1, d, dff)), const((1, 1, dff)), const((1, dff, d)), const((1, 1, d)),
                  const((1, 1, d)), const((1, 1, d))],
        out_specs=pl.BlockSpec(blk, xmap),
        out_shape=jax.ShapeDtypeStruct((a, r, d), F32),
        compiler_params=_cparams(*sem),
        name=f"outmlp_{tag}_{l}",
    )(x3, yc2, ya2, mod, mod, mod, mod, oga3, w_out_bf, ln1g3, ln1b3, w1_bf, b13, w2_bf, b23, ln2g3, ln2b3)


def kernel(x_prompt, x_sample, cache_k, cache_v, state_conv, page_table, c_prompt, c_sample, ln0_g, ln0_b, w_ada, b_ada, w_in, w_dw, b_dw, conv_ln_g, conv_ln_b, out_g_conv, out_g_attn, w_out, ln1_g, ln1_b, w1, b1, w2, b2, ln2_g, ln2_b):
    b, s, d = x_prompt.shape
    bd, ln, _ = x_sample.shape
    depth = w_in.shape[0]
    dc = w_dw.shape[-1]
    da = out_g_attn.shape[-1]
    nh = da // HEAD_DIM
    n_pool = cache_k.shape[1]
    n_pages = page_table.shape[1]
    alpha = (2 * depth) ** 0.25
    assert (n_pages * PAGE_SIZE) % MOBA_BLOCK == 0 and ln <= MOBA_BLOCK and ln == SUBLANES
    assert s % MOBA_BLOCK == 0 and s // MOBA_BLOCK <= HEAD_DIM and da % LANES == 0
    tm = min(PROMPT_ROWS, s)
    tb = min(SAMPLE_SEQS, bd)

    w_ada_bf, w_in_bf, w_out_bf = w_ada.astype(BF16), w_in.astype(BF16), w_out.astype(BF16)
    w1_bf, w2_bf = w1.astype(BF16), w2.astype(BF16)
    w_qkv_t = w_in[:, :, 2 * dc:].transpose(0, 2, 1).astype(BF16)

    def vec3(a):
        return a.reshape(depth, 1, a.shape[-1])

    b_dw3, clg3, clb3, ogc3, oga3 = vec3(b_dw), vec3(conv_ln_g), vec3(conv_ln_b), vec3(out_g_conv), vec3(out_g_attn)
    ln1g3, ln1b3, ln2g3, ln2b3, b13, b23 = vec3(ln1_g), vec3(ln1_b), vec3(ln2_g), vec3(ln2_b), vec3(b1), vec3(b2)

    n_rows = bd + SUBLANES
    assert b <= SUBLANES and bd % SUBLANES == 0
    c_all = jnp.concatenate([c_sample, c_prompt, jnp.zeros((SUBLANES - b, d), F32)], axis=0)
    mod = _ada(c_all, w_ada_bf, b_ada)
    mod_row0 = n_rows - SUBLANES

    xp = _input_ln(x_prompt.reshape(b * s, d), ln0_g, ln0_b, tm).reshape(b, s, d)
    xs = _input_ln(x_sample.reshape(bd * ln, d), ln0_g, ln0_b, min(bd * ln, 512)).reshape(bd, ln, d)

    state_pad = jnp.pad(state_conv, ((0, 0), (0, 0), (CONV_HALO - (CONV_WIDTH - 1), 0), (0, 0)))
    ck2 = cache_k.transpose(0, 1, 3, 4, 2).reshape(depth * n_pool, da, PAGE_SIZE)
    cv2 = cache_v.transpose(0, 1, 3, 4, 2).reshape(depth * n_pool, da, PAGE_SIZE)
    pt_flat = page_table.reshape(-1).astype(jnp.int32)

    kp, vp, cp, ksm, vsm, csm = [], [], [], [], [], []
    for l in range(depth):
        yc, qt2, kt2, vt2, kb2, vbt2, km3, cn = _inproj_prompt(l, xp, mod, w_in_bf, w_qkv_t, w_dw, b_dw3, clg3,
                                                              clb3, ogc3, tm)
        ya = _attn_prompt(l, qt2, kb2, vbt2, km3, b, s)
        xp = _outmlp(l, xp, yc, ya, mod, mod_row0, False, oga3, w_out_bf, ln1g3, ln1b3, w1_bf, b13, w2_bf, b23,
                     ln2g3, ln2b3, alpha, tm, "prompt")
        kp.append(kt2.reshape(b, nh, HEAD_DIM, s))
        vp.append(vt2.reshape(b, nh, HEAD_DIM, s))
        cp.append(cn[:, CONV_HALO - (CONV_WIDTH - 1):])

        ycs, qs, ks, vs, cns = _inproj_sample(l, xs, mod, state_pad, w_in_bf, w_dw, b_dw3, clg3, clb3, ogc3, tb)
        yas = _attn_sample(l, qs.reshape(bd, ln, da), ks.reshape(bd, ln, da), vs.reshape(bd, ln, da),
                           ck2, cv2, pt_flat, n_pool, n_pages)
        xs = _outmlp(l, xs, ycs, yas.reshape(bd * ln, da), mod, 0, True, oga3, w_out_bf, ln1g3, ln1b3, w1_bf,
                     b13, w2_bf, b23, ln2g3, ln2b3, alpha, tb, "sample")
        ksm.append(ks.reshape(bd, ln, nh, HEAD_DIM))
        vsm.append(vs.reshape(bd, ln, nh, HEAD_DIM))
        csm.append(cns[:, CONV_HALO - (CONV_WIDTH - 1):])

    k_prompt = jnp.stack(kp).transpose(0, 1, 4, 2, 3)
    v_prompt = jnp.stack(vp).transpose(0, 1, 4, 2, 3)
    return (xp, xs, k_prompt, v_prompt, jnp.stack(cp), jnp.stack(ksm), jnp.stack(vsm), jnp.stack(csm))
```

```python
import functools

import jax
import jax.numpy as jnp
from jax import lax
from jax.experimental import pallas as pl
from jax.experimental.pallas import tpu as pltpu

F32 = jnp.float32
BF16 = jnp.bfloat16

LN_EPS = 1e-5
HEAD_DIM = 64
MOBA_BLOCK = 256
MOBA_TOPK = 3
CONV_WIDTH = 31
PAGE_SIZE = 128
N_MOD = 6
SH1, SC1, G1, SH2, SC2, G2 = range(N_MOD)

SUBLANES = 8
LANES = 128
CONV_HALO = 32
CONV_ROWS = 64
NEG_BIG = -1e30
LOG2_E = 1.4426950408889634
VMEM_LIMIT = 56 * 1024 * 1024

PROMPT_ROWS = 512
SAMPLE_SEQS = 32
KV_TILE_BLOCKS = 4
SAMPLE_ATTN_SEQS = 2


def _cparams(*sem):
    return pltpu.CompilerParams(dimension_semantics=sem, vmem_limit_bytes=VMEM_LIMIT)


def _layer_norm(x, g, b):
    mu = jnp.mean(x, axis=-1, keepdims=True)
    xc = x - mu
    var = jnp.mean(xc * xc, axis=-1, keepdims=True)
    return xc * lax.rsqrt(var + LN_EPS) * g + b


def _rms_scale(x, g):
    return x * lax.rsqrt(jnp.mean(x * x, axis=-1, keepdims=True) + LN_EPS) * g


def _dot(a, b):
    return jnp.dot(a, b, preferred_element_type=F32)


def _dot_t(a, b, precision=None):
    return lax.dot_general(a, b, (((1,), (1,)), ((), ())), preferred_element_type=F32, precision=precision)


def _ln_kernel(x_ref, g_ref, b_ref, o_ref):
    o_ref[...] = _layer_norm(x_ref[...], g_ref[...], b_ref[...])


def _input_ln(x2, g, b, rows):
    n, d = x2.shape
    return pl.pallas_call(
        _ln_kernel,
        grid=(n // rows,),
        in_specs=[pl.BlockSpec((rows, d), lambda i: (i, 0)),
                  pl.BlockSpec((1, d), lambda i: (0, 0)),
                  pl.BlockSpec((1, d), lambda i: (0, 0))],
        out_specs=pl.BlockSpec((rows, d), lambda i: (i, 0)),
        out_shape=jax.ShapeDtypeStruct((n, d), F32),
        compiler_params=_cparams("parallel"),
        name="input_ln",
    )(x2, g.reshape(1, d), b.reshape(1, d))


def _ada_kernel(c_ref, w_ref, b_ref, o_ref):
    c = c_ref[...]
    h = (c * jax.nn.sigmoid(c)).astype(BF16)
    o_ref[0] = _dot(h, w_ref[0]) + b_ref[0]


def _ada(c_all, w_ada_bf, b_ada):
    depth, d, _ = w_ada_bf.shape
    r = c_all.shape[0]
    out = pl.pallas_call(
        _ada_kernel,
        grid=(depth, N_MOD),
        in_specs=[pl.BlockSpec((r, d), lambda l, j: (0, 0)),
                  pl.BlockSpec((1, d, d), lambda l, j: (l, 0, j)),
                  pl.BlockSpec((1, 1, d), lambda l, j: (l * N_MOD + j, 0, 0))],
        out_specs=pl.BlockSpec((1, r, d), lambda l, j: (l * N_MOD + j, 0, 0)),
        out_shape=jax.ShapeDtypeStruct((depth * N_MOD, r, d), F32),
        compiler_params=_cparams("parallel", "parallel"),
        name="ada_mod",
    )(c_all, w_ada_bf, b_ada.reshape(depth * N_MOD, 1, d))
    return out.reshape(depth * N_MOD, r, 1, d)


def _conv_post(y, clg, clb, ogc):
    yn = _layer_norm(y, clg, clb)
    ys = yn * jax.nn.sigmoid(yn)
    return _rms_scale(ys, ogc).astype(BF16)


def _inproj_prompt_kernel(x_ref, sc_ref, sh_ref, w_ref, wt_ref, wdw_ref, bdw_ref, clg_ref, clb_ref, ogc_ref,
                          yc_ref, qt_ref, kt_ref, vt_ref, kb_ref, vbt_ref, km_ref, cn_ref, ubuf, ushift,
                          *, tm, dc, da):
    i = pl.program_id(1)

    @pl.when(i == 0)
    def _():
        ubuf[0:CONV_HALO, :] = jnp.zeros((CONV_HALO, dc), F32)

    x = x_ref[0]
    h = (x * (1.0 + sc_ref[0, 0]) + sh_ref[0, 0]).astype(BF16)

    zc = _dot(h, w_ref[0, :, 0:2 * dc])
    ubuf[CONV_HALO:CONV_HALO + tm, :] = zc[:, :dc] * jax.nn.sigmoid(zc[:, dc:])

    first = CONV_HALO - (CONV_WIDTH - 1)
    span = tm + CONV_HALO - SUBLANES
    for r in range(1, SUBLANES):
        ushift[r - 1] = ubuf[r:r + span, :]
    for c in range(tm // CONV_ROWS):
        groups = []
        for g in range(dc // LANES):
            ls = slice(g * LANES, (g + 1) * LANES)
            acc = jnp.broadcast_to(bdw_ref[0, :, ls], (CONV_ROWS, LANES))
            for j in range(CONV_WIDTH):
                a, r = divmod(first + j, SUBLANES)
                r0 = c * CONV_ROWS + a * SUBLANES
                src = ubuf[r0:r0 + CONV_ROWS, ls] if r == 0 else ushift[r - 1, r0:r0 + CONV_ROWS, ls]
                acc = acc + wdw_ref[0, j:j + 1, ls] * src
            groups.append(acc)
        y = jnp.concatenate(groups, axis=1)
        yc_ref[c * CONV_ROWS:(c + 1) * CONV_ROWS, :] = _conv_post(y, clg_ref[0], clb_ref[0], ogc_ref[0])

    tail = ubuf[tm:tm + CONV_HALO, :]
    cn_ref[0] = tail
    ubuf[0:CONV_HALO, :] = tail

    qt_ref[...] = _dot_t(wt_ref[0, 0:da, :], h) * (HEAD_DIM ** -0.5 * LOG2_E)
    kt_ref[...] = _dot_t(wt_ref[0, da:2 * da, :], h)
    zvt = _dot_t(wt_ref[0, 2 * da:3 * da, :], h)
    vt_ref[...] = zvt
    vbt_ref[...] = zvt.astype(BF16)
    c0 = 2 * dc
    zk = _dot(h, w_ref[0, :, c0 + da:c0 + 2 * da])
    kb_ref[...] = zk.astype(BF16)
    for r in range(tm // MOBA_BLOCK):
        km_ref[0, r:r + 1, :] = jnp.mean(zk[r * MOBA_BLOCK:(r + 1) * MOBA_BLOCK], axis=0, keepdims=True)


def _inproj_prompt_kernel_kv(*refs, n_in, **kw):
    _inproj_prompt_kernel(*refs[:n_in], *refs[n_in + 2:], **kw)


def _inproj_prompt(l, depth, x3, mod, w_in_bf, w_qkv_t, w_dw, b_dw3, clg3, clb3, ogc3, tm, kv_all):
    b, s, d = x3.shape
    dc = w_dw.shape[-1]
    da = (w_in_bf.shape[-1] - 2 * dc) // 3
    r = mod.shape[1]
    nt = s // tm
    nbt = tm // MOBA_BLOCK
    n = b * s
    prow = r - SUBLANES
    kw = dict(tm=tm, dc=dc, da=da)

    def vec(a):
        return pl.BlockSpec((1, 1, a.shape[-1]), lambda bi, i: (l, 0, 0))

    def rows(width):
        return pl.BlockSpec((tm, width), lambda bi, i: (bi * nt + i, 0))

    cols = pl.BlockSpec((da, tm), lambda bi, i: (bi, i))
    cols_l = pl.BlockSpec((da, tm), lambda bi, i: (l * b + bi, i))

    in_specs = [pl.BlockSpec((1, tm, d), lambda bi, i: (bi, i, 0)),
                pl.BlockSpec((1, 1, 1, d), lambda bi, i: (l * N_MOD + SC1, prow + bi, 0, 0)),
                pl.BlockSpec((1, 1, 1, d), lambda bi, i: (l * N_MOD + SH1, prow + bi, 0, 0)),
                pl.BlockSpec((1, d, w_in_bf.shape[-1]), lambda bi, i: (l, 0, 0)),
                pl.BlockSpec((1, 3 * da, d), lambda bi, i: (l, 0, 0)),
                pl.BlockSpec((1, CONV_WIDTH, dc), lambda bi, i: (l, 0, 0)),
                vec(b_dw3), vec(clg3), vec(clb3), vec(ogc3)]
    args = [x3, mod, mod, w_in_bf, w_qkv_t, w_dw, b_dw3, clg3, clb3, ogc3]
    n_in = len(args)
    if kv_all is None:
        kern, aliases = functools.partial(_inproj_prompt_kernel, **kw), {}
    else:
        kern = functools.partial(_inproj_prompt_kernel_kv, n_in=n_in, **kw)
        in_specs += [pl.BlockSpec(memory_space=pl.ANY)] * 2
        args += list(kv_all)
        aliases = {n_in: 2, n_in + 1: 3}

    return pl.pallas_call(
        kern,
        grid=(b, nt),
        in_specs=in_specs,
        out_specs=[rows(dc), cols, cols_l, cols_l, rows(da), cols,
                   pl.BlockSpec((1, nbt, da), lambda bi, i: (bi * nt + i, 0, 0)),
                   pl.BlockSpec((1, CONV_HALO, dc), lambda bi, i: (bi, 0, 0))],
        out_shape=[jax.ShapeDtypeStruct((n, dc), BF16),
                   jax.ShapeDtypeStruct((b * da, s), F32),
                   jax.ShapeDtypeStruct((depth * b * da, s), F32),
                   jax.ShapeDtypeStruct((depth * b * da, s), F32),
                   jax.ShapeDtypeStruct((n, da), BF16),
                   jax.ShapeDtypeStruct((b * da, s), BF16),
                   jax.ShapeDtypeStruct((b * nt, nbt, da), F32),
                   jax.ShapeDtypeStruct((b, CONV_HALO, dc), F32)],
        scratch_shapes=[pltpu.VMEM((CONV_HALO + tm, dc), F32),
                        pltpu.VMEM((SUBLANES - 1, tm + CONV_HALO - SUBLANES, dc), F32)],
        input_output_aliases=aliases,
        compiler_params=_cparams("parallel", "arbitrary"),
        name=f"inproj_prompt_{l}",
    )(*args)


def _inproj_sample_kernel(x_ref, sc_ref, sh_ref, st_ref, w_ref, wdw_ref, bdw_ref, clg_ref, clb_ref, ogc_ref,
                          yc_ref, q_ref, k_ref, v_ref, cn_ref, uext, *, tb, ln, dc, da):
    d = x_ref.shape[-1]
    h = (x_ref[...] * (1.0 + sc_ref[0]) + sh_ref[0]).reshape(tb * ln, d).astype(BF16)

    zc = _dot(h, w_ref[0, :, 0:2 * dc])
    u = zc[:, :dc] * jax.nn.sigmoid(zc[:, dc:])
    uext[:, 0:CONV_HALO, :] = st_ref[...]
    uext[:, CONV_HALO:CONV_HALO + ln, :] = u.reshape(tb, ln, dc)

    first = CONV_HALO - (CONV_WIDTH - 1)
    acc = jnp.broadcast_to(bdw_ref[0], (tb, ln, dc))
    for j in range(CONV_WIDTH):
        acc = acc + wdw_ref[0, j:j + 1, :] * uext[:, first + j:first + j + ln, :]
    yc_ref[...] = _conv_post(acc.reshape(tb * ln, dc), clg_ref[0], clb_ref[0], ogc_ref[0])
    cn_ref[...] = uext[:, ln:ln + CONV_HALO, :]

    c0 = 2 * dc
    q_ref[...] = _dot(h, w_ref[0, :, c0:c0 + da]) * (HEAD_DIM ** -0.5)
    k_ref[...] = _dot(h, w_ref[0, :, c0 + da:c0 + 2 * da])
    v_ref[...] = _dot(h, w_ref[0, :, c0 + 2 * da:c0 + 3 * da])


def _inproj_sample(l, x3, mod, state_pad, w_in_bf, w_dw, b_dw3, clg3, clb3, ogc3, tb):
    bd, ln, d = x3.shape
    dc = w_dw.shape[-1]
    da = (w_in_bf.shape[-1] - 2 * dc) // 3
    n = bd * ln
    kern = functools.partial(_inproj_sample_kernel, tb=tb, ln=ln, dc=dc, da=da)

    def vec(a):
        return pl.BlockSpec((1, 1, a.shape[-1]), lambda i: (l, 0, 0))

    def rows(width):
        return pl.BlockSpec((tb * ln, width), lambda i: (i, 0))

    return pl.pallas_call(
        kern,
        grid=(bd // tb,),
        in_specs=[pl.BlockSpec((tb, ln, d), lambda i: (i, 0, 0)),
                  pl.BlockSpec((1, tb, 1, d), lambda i: (l * N_MOD + SC1, i, 0, 0)),
                  pl.BlockSpec((1, tb, 1, d), lambda i: (l * N_MOD + SH1, i, 0, 0)),
                  pl.BlockSpec((None, tb, CONV_HALO, dc), lambda i: (l, i, 0, 0)),
                  pl.BlockSpec((1, d, w_in_bf.shape[-1]), lambda i: (l, 0, 0)),
                  pl.BlockSpec((1, CONV_WIDTH, dc), lambda i: (l, 0, 0)),
                  vec(b_dw3), vec(clg3), vec(clb3), vec(ogc3)],
        out_specs=[rows(dc), rows(da), rows(da), rows(da),
                   pl.BlockSpec((tb, CONV_HALO, dc), lambda i: (i, 0, 0))],
        out_shape=[jax.ShapeDtypeStruct((n, dc), BF16),
                   jax.ShapeDtypeStruct((n, da), F32),
                   jax.ShapeDtypeStruct((n, da), F32),
                   jax.ShapeDtypeStruct((n, da), F32),
                   jax.ShapeDtypeStruct((bd, CONV_HALO, dc), F32)],
        scratch_shapes=[pltpu.VMEM((tb, CONV_HALO + ln, dc), F32)],
        compiler_params=_cparams("parallel"),
        name=f"inproj_sample_{l}",
    )(x3, mod, mod, state_pad, w_in_bf, w_dw, b_dw3, clg3, clb3, ogc3)


def _topk_mask(gate, valid, pos, k, axis):
    g = jnp.where(valid, gate, -jnp.inf)
    sel = jnp.zeros(gate.shape, jnp.bool_)
    for _ in range(k):
        mx = jnp.max(g, axis=axis, keepdims=True)
        idx = jnp.min(jnp.where(g == mx, pos, float(gate.shape[axis])), axis=axis, keepdims=True)
        pick = (pos == idx) & (mx > -jnp.inf)
        sel = sel | pick
        g = jnp.where(pick, -jnp.inf, g)
    return sel


def _attn_prompt_kernel(qt_ref, kb_ref, vbt_ref, km_ref, o_ref, kaug, qaug, s_a, s_b, m_sc, l_sc, acc_sc,
                        *, nb, kvb):
    tq = MOBA_BLOCK
    kc = kvb * MOBA_BLOCK
    lane = lax.broadcasted_iota(jnp.int32, (tq, LANES), 1)
    head_lanes = (lane < HEAD_DIM, lane >= HEAD_DIM)
    pen_base = (HEAD_DIM, 0)

    for n in range(nb):
        kblk = kb_ref[n * tq:(n + 1) * tq, :]
        for hh in range(2):
            onehot = jnp.where(lane == pen_base[hh] + n, 1.0, 0.0).astype(BF16)
            kaug[hh, n * tq:(n + 1) * tq, :] = jnp.where(head_lanes[hh], kblk, onehot)

    km = km_ref[0]
    lane_nb = lax.broadcasted_iota(jnp.int32, (nb, LANES), 1)
    km2 = jnp.concatenate([jnp.where(lane_nb < HEAD_DIM, km, 0.0), jnp.where(lane_nb >= HEAD_DIM, km, 0.0)], axis=0)

    def scores(j, c, buf, causal):
        k0 = pl.multiple_of(c * kc, kc)
        for hh in range(2):
            for sb in range(kvb):
                ks = k0 + sb * tq
                buf[hh, sb * tq:(sb + 1) * tq, :] = _dot(kaug[hh, pl.ds(ks, tq), :], qaug[hh])
        if causal:
            own = pl.multiple_of((j % kvb) * tq, tq)
            key_i = lax.broadcasted_iota(jnp.int32, (tq, tq), 0)
            qry_i = lax.broadcasted_iota(jnp.int32, (tq, tq), 1)
            for hh in range(2):
                buf[hh, pl.ds(own, tq), :] = jnp.where(key_i <= qry_i, buf[hh, pl.ds(own, tq), :], NEG_BIG)

    def softmax_pv(c, buf):
        k0 = pl.multiple_of(c * kc, kc)
        m_hd = []
        for hh in range(2):
            m_cur = jnp.max(buf[hh, 0:tq, :], axis=0, keepdims=True)
            for sb in range(1, kvb):
                m_cur = jnp.maximum(m_cur, jnp.max(buf[hh, sb * tq:(sb + 1) * tq, :], axis=0, keepdims=True))
            m_hd.append(jnp.maximum(m_sc[hh], m_cur))
        for hh in range(2):
            m_row = m_hd[hh][0:1]
            alpha = jnp.exp2(m_sc[hh] - m_hd[hh])
            p_parts = [jnp.exp2(buf[hh, sb * tq:(sb + 1) * tq, :] - m_row) for sb in range(kvb)]
            psum = jnp.sum(p_parts[0], axis=0, keepdims=True)
            for sb in range(1, kvb):
                psum = psum + jnp.sum(p_parts[sb], axis=0, keepdims=True)
            l_sc[hh] = alpha * l_sc[hh] + psum
            v_t = vbt_ref[hh * HEAD_DIM:(hh + 1) * HEAD_DIM, pl.ds(k0, kc)]
            p_all = jnp.concatenate([p.astype(BF16) for p in p_parts], axis=0)
            acc_sc[hh] = alpha[0:1] * acc_sc[hh] + _dot(v_t, p_all)
            m_sc[hh] = m_hd[hh]

    def query_block(j, carry):
        j0 = pl.multiple_of(j * tq, tq)
        qt = qt_ref[:, pl.ds(j0, tq)]
        gate2 = jnp.dot(km2, qt, preferred_element_type=F32, precision=lax.Precision.HIGHEST)
        blk_i = lax.broadcasted_iota(jnp.int32, (nb, tq), 0)
        blk_f = blk_i.astype(F32)
        zfill = jnp.zeros((HEAD_DIM - nb, tq), F32)
        for hh in range(2):
            sel = _topk_mask(gate2[hh * nb:(hh + 1) * nb], blk_i < j, blk_f, min(MOBA_TOPK, nb), 0)
            pen_t = jnp.where(sel | (blk_i == j), 0.0, NEG_BIG)
            if hh == 0:
                rows_t = [qt[0:HEAD_DIM], pen_t, zfill]
            else:
                rows_t = [pen_t, zfill, qt[HEAD_DIM:2 * HEAD_DIM]]
            qaug[hh] = jnp.concatenate(rows_t, axis=0).astype(BF16)

        m_sc[...] = jnp.full(m_sc.shape, -jnp.inf, F32)
        l_sc[...] = jnp.zeros(l_sc.shape, F32)
        acc_sc[...] = jnp.zeros(acc_sc.shape, F32)

        c_own = j // kvb
        bufs = (s_a, s_b)
        scores(j, c_own, bufs[0], True)
        for i in range(nb // kvb):
            @pl.when(i <= c_own)
            def _():
                scores(j, jnp.maximum(c_own - i - 1, 0), bufs[(i + 1) % 2], False)
                softmax_pv(c_own - i, bufs[i % 2])

        o_t = jnp.concatenate([acc_sc[0] / l_sc[0, 0:1], acc_sc[1] / l_sc[1, 0:1]], axis=0)
        o_ref[pl.ds(j0, tq), :] = o_t.T
        return carry

    lax.fori_loop(0, nb, query_block, 0)


def _attn_prompt(l, qt2, kb2, vbt2, km3, b, s):
    n, da = kb2.shape
    nb = s // MOBA_BLOCK
    hp = da // LANES
    tq = MOBA_BLOCK
    kvb = max(c for c in range(1, KV_TILE_BLOCKS + 1) if nb % c == 0)
    km3 = km3.reshape(b, nb, da)
    kern = functools.partial(_attn_prompt_kernel, nb=nb, kvb=kvb)
    return pl.pallas_call(
        kern,
        grid=(b, hp),
        in_specs=[pl.BlockSpec((LANES, s), lambda bi, p: (bi * hp + p, 0)),
                  pl.BlockSpec((s, LANES), lambda bi, p: (bi, p)),
                  pl.BlockSpec((LANES, s), lambda bi, p: (bi * hp + p, 0)),
                  pl.BlockSpec((1, nb, LANES), lambda bi, p: (bi, 0, p))],
        out_specs=pl.BlockSpec((s, LANES), lambda bi, p: (bi, p)),
        out_shape=jax.ShapeDtypeStruct((n, da), F32),
        scratch_shapes=[pltpu.VMEM((2, s, LANES), BF16),
                        pltpu.VMEM((2, LANES, tq), BF16),
                        pltpu.VMEM((2, kvb * tq, tq), F32),
                        pltpu.VMEM((2, kvb * tq, tq), F32),
                        pltpu.VMEM((2, SUBLANES, tq), F32),
                        pltpu.VMEM((2, SUBLANES, tq), F32),
                        pltpu.VMEM((2, HEAD_DIM, tq), F32)],
        compiler_params=_cparams("parallel", "parallel"),
        name=f"attn_prompt_{l}",
    )(qt2, kb2, vbt2, km3)


def _attn_sample_kernel(pt_ref, q_ref, kn_ref, vn_ref, *refs, n_pages, nh, ln, spq):
    del pt_ref
    o_ref, kbuf, vbuf = refs[2 * spq * n_pages:]
    for sq in range(spq):
        k_pages = refs[sq * n_pages:(sq + 1) * n_pages]
        v_pages = refs[(spq + sq) * n_pages:(spq + sq + 1) * n_pages]
        o_ref[sq] = _attn_sample_one(q_ref[sq], kn_ref[sq], vn_ref[sq], k_pages, v_pages, kbuf.at[sq], vbuf.at[sq],
                                     nh=nh, ln=ln)


def _attn_sample_one(q, kn_new, vn_new, k_pages, v_pages, kbuf, vbuf, *, nh, ln):
    n_pages = len(k_pages)
    da = nh * HEAD_DIM
    rows = nh * ln
    ppb = MOBA_BLOCK // PAGE_SIZE
    nbp = n_pages // ppb

    qt = jnp.concatenate([q] * nh, axis=0)
    row = lax.broadcasted_iota(jnp.int32, (rows, da), 0)
    lane = lax.broadcasted_iota(jnp.int32, (rows, da), 1)
    own_head = (lane // HEAD_DIM) == (row // ln)
    qbd = jnp.where(own_head, qt, 0.0)
    qbd_bf = qbd.astype(BF16)

    km_lane = lax.broadcasted_iota(jnp.int32, (da, LANES), 1)
    km_t = jnp.zeros((da, LANES), F32)
    for n in range(nbp):
        ksum = jnp.zeros((da, PAGE_SIZE), F32)
        for pg in range(ppb):
            p_i = n * ppb + pg
            kp = k_pages[p_i][0]
            ksum = ksum + kp
            kbuf[:, p_i * PAGE_SIZE:(p_i + 1) * PAGE_SIZE] = kp.astype(BF16)
            vbuf[:, p_i * PAGE_SIZE:(p_i + 1) * PAGE_SIZE] = v_pages[p_i][0].astype(BF16)
        kmean = jnp.sum(ksum, axis=1, keepdims=True) * (1.0 / MOBA_BLOCK)
        km_t = jnp.where(km_lane == n, kmean, km_t)

    gate = jnp.dot(qbd, km_t, preferred_element_type=F32, precision=lax.Precision.HIGHEST)
    glane = lax.broadcasted_iota(jnp.int32, (rows, LANES), 1)
    sel = _topk_mask(gate, glane < nbp, glane.astype(F32), min(MOBA_TOPK, nbp), 1)
    bias = jnp.where(sel, 0.0, NEG_BIG)

    zrows = jnp.zeros((ln, da), F32)
    kn = jnp.concatenate([kn_new, zrows], axis=0).astype(BF16)
    vn = jnp.concatenate([vn_new, zrows], axis=0).astype(BF16)
    s_own = _dot_t(qbd_bf, kn)
    orow = lax.broadcasted_iota(jnp.int32, (rows, 2 * ln), 0)
    ocol = lax.broadcasted_iota(jnp.int32, (rows, 2 * ln), 1)
    s_own = jnp.where(ocol <= orow % ln, s_own, NEG_BIG)
    m = jnp.max(s_own, axis=-1, keepdims=True)

    s_all = _dot(qbd_bf, kbuf[...])
    s_past = []
    for n in range(nbp):
        s = s_all[:, n * MOBA_BLOCK:(n + 1) * MOBA_BLOCK] + bias[:, n:n + 1]
        s_past.append(s)
        m = jnp.maximum(m, jnp.max(s, axis=-1, keepdims=True))

    p_own = jnp.exp(s_own - m)
    lsum = jnp.sum(p_own, axis=-1, keepdims=True)
    p_past = []
    for n in range(nbp):
        p = jnp.exp(s_past[n] - m)
        lsum = lsum + jnp.sum(p, axis=-1, keepdims=True)
        p_past.append(p.astype(BF16))
    acc = _dot(p_own.astype(BF16), vn) + _dot_t(jnp.concatenate(p_past, axis=1), vbuf[...])

    o = jnp.where(own_head, acc / lsum, 0.0)
    out = o[0:ln]
    for hh in range(1, nh):
        out = out + o[hh * ln:(hh + 1) * ln]
    return out


def _attn_sample(l, q3, k3, v3, ck_t, cv_t, pt_flat, n_pool, n_pages):
    bd, ln, da = q3.shape
    nh = da // HEAD_DIM
    spq = SAMPLE_ATTN_SEQS if bd % SAMPLE_ATTN_SEQS == 0 else 1
    kern = functools.partial(_attn_sample_kernel, n_pages=n_pages, nh=nh, ln=ln, spq=spq)

    def page_spec(sq, p_i):
        return pl.BlockSpec((1, da, PAGE_SIZE),
                            lambda i, pt: (l * n_pool + pt[(i * spq + sq) * n_pages + p_i], 0, 0))

    pages = [page_spec(sq, p_i) for sq in range(spq) for p_i in range(n_pages)]
    tok = pl.BlockSpec((spq, ln, da), lambda i, pt: (i, 0, 0))
    grid_spec = pltpu.PrefetchScalarGridSpec(
        num_scalar_prefetch=1,
        grid=(bd // spq,),
        in_specs=[tok, tok, tok] + pages * 2,
        out_specs=tok,
        scratch_shapes=[pltpu.VMEM((spq, da, n_pages * PAGE_SIZE), BF16),
                        pltpu.VMEM((spq, da, n_pages * PAGE_SIZE), BF16)],
    )
    return pl.pallas_call(
        kern,
        grid_spec=grid_spec,
        out_shape=jax.ShapeDtypeStruct((bd, ln, da), F32),
        compiler_params=_cparams("arbitrary"),
        name=f"attn_sample_{l}",
    )(pt_flat, q3, k3, v3, *([ck_t] * (spq * n_pages)), *([cv_t] * (spq * n_pages)))


def _outmlp_kernel(x_ref, yc_ref, ya_ref, g1_ref, sc2_ref, sh2_ref, g2_ref, oga_ref, wout_ref,
                   ln1g_ref, ln1b_ref, w1_ref, b1_ref, w2_ref, b2_ref, ln2g_ref, ln2b_ref, o_ref,
                   *, alpha, ff_chunk):
    x = x_ref[...]
    blk = x.shape
    rows, d = blk[0] * blk[1], blk[2]
    dc = yc_ref.shape[-1]
    dff = w1_ref.shape[-1]

    yan = _rms_scale(ya_ref[...], oga_ref[0]).astype(BF16)
    mix = _dot(yc_ref[...], wout_ref[0, 0:dc, :]) + _dot(yan, wout_ref[0, dc:, :])
    x1 = _layer_norm(alpha * x + (1.0 + g1_ref[0]) * mix.reshape(blk), ln1g_ref[0], ln1b_ref[0])

    h2 = (x1 * (1.0 + sc2_ref[0]) + sh2_ref[0]).reshape(rows, d).astype(BF16)
    f = jnp.broadcast_to(b2_ref[0], (rows, d))
    for c in range(dff // ff_chunk):
        cs = slice(c * ff_chunk, (c + 1) * ff_chunk)
        hid = jnp.maximum(_dot(h2, w1_ref[0, :, cs]) + b1_ref[0, :, cs], 0.0)
        f = f + _dot((hid * hid).astype(BF16), w2_ref[0, cs, :])
    o_ref[...] = _layer_norm(alpha * x1 + (1.0 + g2_ref[0]) * f.reshape(blk), ln2g_ref[0], ln2b_ref[0])


def _outmlp(l, x3, yc2, ya2, mod, mod_row0, per_row_mod, oga3, w_out_bf, ln1g3, ln1b3, w1_bf, b13, w2_bf,
            b23, ln2g3, ln2b3, alpha, grp, tag):
    a, r, d = x3.shape
    dc, da = yc2.shape[-1], ya2.shape[-1]
    dff = w1_bf.shape[-1]
    if per_row_mod:
        blk = (grp, r, d)
        grid = (a // grp,)
        xmap = lambda i: (i, 0, 0)
        rmap = lambda i: (i, 0)
        brows = grp * r
        mblk = (1, grp, 1, d)

        def mod_spec(comp):
            return pl.BlockSpec(mblk, lambda i: (l * N_MOD + comp, i, 0, 0))
        sem = ("parallel",)
    else:
        nt = r // grp
        blk = (1, grp, d)
        grid = (a, nt)
        xmap = lambda bi, i: (bi, i, 0)
        rmap = lambda bi, i: (bi * nt + i, 0)
        brows = grp
        mblk = (1, 1, 1, d)

        def mod_spec(comp):
            return pl.BlockSpec(mblk, lambda bi, i: (l * N_MOD + comp, mod_row0 + bi, 0, 0))
        sem = ("parallel", "parallel")

    nargs = len(grid)

    def const(shape):
        zeros = (0,) * (len(shape) - 1)
        if nargs == 1:
            return pl.BlockSpec(shape, lambda i: (l,) + zeros, pipeline_mode=pl.Buffered(1))
        return pl.BlockSpec(shape, lambda bi, i: (l,) + zeros, pipeline_mode=pl.Buffered(1))

    kern = functools.partial(_outmlp_kernel, alpha=alpha, ff_chunk=min(dff, 1024))
    return pl.pallas_call(
        kern,
        grid=grid,
        in_specs=[pl.BlockSpec(blk, xmap),
                  pl.BlockSpec((brows, dc), rmap),
                  pl.BlockSpec((brows, da), rmap),
                  mod_spec(G1), mod_spec(SC2), mod_spec(SH2), mod_spec(G2),
                  const((1, 1, da)), const((1, dc + da, d)),
                  const((1, 1, d)), const((1, 1, d)),
                  const((1, d, dff)), const((1, 1, dff)), const((1, dff, d)), const((1, 1, d)),
                  const((1, 1, d)), const((1, 1, d))],
        out_specs=pl.BlockSpec(blk, xmap),
        out_shape=jax.ShapeDtypeStruct((a, r, d), F32),
        compiler_params=_cparams(*sem),
        name=f"outmlp_{tag}_{l}",
    )(x3, yc2, ya2, mod, mod, mod, mod, oga3, w_out_bf, ln1g3, ln1b3, w1_bf, b13, w2_bf, b23, ln2g3, ln2b3)


def kernel(x_prompt, x_sample, cache_k, cache_v, state_conv, page_table, c_prompt, c_sample, ln0_g, ln0_b, w_ada, b_ada, w_in, w_dw, b_dw, conv_ln_g, conv_ln_b, out_g_conv, out_g_attn, w_out, ln1_g, ln1_b, w1, b1, w2, b2, ln2_g, ln2_b):
    b, s, d = x_prompt.shape
    bd, ln, _ = x_sample.shape
    depth = w_in.shape[0]
    dc = w_dw.shape[-1]
    da = out_g_attn.shape[-1]
    nh = da // HEAD_DIM
    n_pool = cache_k.shape[1]
    n_pages = page_table.shape[1]
    alpha = (2 * depth) ** 0.25
    assert (n_pages * PAGE_SIZE) % MOBA_BLOCK == 0 and ln <= MOBA_BLOCK and ln == SUBLANES
    assert s % MOBA_BLOCK == 0 and s // MOBA_BLOCK <= HEAD_DIM and da % LANES == 0
    tm = min(PROMPT_ROWS, s)
    tb = min(SAMPLE_SEQS, bd)

    w_ada_bf, w_in_bf, w_out_bf = w_ada.astype(BF16), w_in.astype(BF16), w_out.astype(BF16)
    w1_bf, w2_bf = w1.astype(BF16), w2.astype(BF16)
    w_qkv_t = w_in[:, :, 2 * dc:].transpose(0, 2, 1).astype(BF16)

    def vec3(a):
        return a.reshape(depth, 1, a.shape[-1])

    b_dw3, clg3, clb3, ogc3, oga3 = vec3(b_dw), vec3(conv_ln_g), vec3(conv_ln_b), vec3(out_g_conv), vec3(out_g_attn)
    ln1g3, ln1b3, ln2g3, ln2b3, b13, b23 = vec3(ln1_g), vec3(ln1_b), vec3(ln2_g), vec3(ln2_b), vec3(b1), vec3(b2)

    n_rows = bd + SUBLANES
    assert b <= SUBLANES and bd % SUBLANES == 0
    c_all = jnp.concatenate([c_sample, c_prompt, jnp.zeros((SUBLANES - b, d), F32)], axis=0)
    mod = _ada(c_all, w_ada_bf, b_ada)
    mod_row0 = n_rows - SUBLANES

    xp = _input_ln(x_prompt.reshape(b * s, d), ln0_g, ln0_b, tm).reshape(b, s, d)
    xs = _input_ln(x_sample.reshape(bd * ln, d), ln0_g, ln0_b, min(bd * ln, 512)).reshape(bd, ln, d)

    state_pad = jnp.pad(state_conv, ((0, 0), (0, 0), (CONV_HALO - (CONV_WIDTH - 1), 0), (0, 0)))
    ck2 = cache_k.transpose(0, 1, 3, 4, 2).reshape(depth * n_pool, da, PAGE_SIZE)
    cv2 = cache_v.transpose(0, 1, 3, 4, 2).reshape(depth * n_pool, da, PAGE_SIZE)
    pt_flat = page_table.reshape(-1).astype(jnp.int32)

    cp, ksm, vsm, csm = [], [], [], []
    kt_all = vt_all = None
    for l in range(depth):
        yc, qt2, kt_all, vt_all, kb2, vbt2, km3, cn = _inproj_prompt(
            l, depth, xp, mod, w_in_bf, w_qkv_t, w_dw, b_dw3, clg3, clb3, ogc3, tm,
            None if l == 0 else (kt_all, vt_all))
        ya = _attn_prompt(l, qt2, kb2, vbt2, km3, b, s)
        xp = _outmlp(l, xp, yc, ya, mod, mod_row0, False, oga3, w_out_bf, ln1g3, ln1b3, w1_bf, b13, w2_bf, b23,
                     ln2g3, ln2b3, alpha, tm, "prompt")
        cp.append(cn[:, CONV_HALO - (CONV_WIDTH - 1):])

        ycs, qs, ks, vs, cns = _inproj_sample(l, xs, mod, state_pad, w_in_bf, w_dw, b_dw3, clg3, clb3, ogc3, tb)
        yas = _attn_sample(l, qs.reshape(bd, ln, da), ks.reshape(bd, ln, da), vs.reshape(bd, ln, da),
                           ck2, cv2, pt_flat, n_pool, n_pages)
        xs = _outmlp(l, xs, ycs, yas.reshape(bd * ln, da), mod, 0, True, oga3, w_out_bf, ln1g3, ln1b3, w1_bf,
                     b13, w2_bf, b23, ln2g3, ln2b3, alpha, tb, "sample")
        ksm.append(ks.reshape(bd, ln, nh, HEAD_DIM))
        vsm.append(vs.reshape(bd, ln, nh, HEAD_DIM))
        csm.append(cns[:, CONV_HALO - (CONV_WIDTH - 1):])

    k_prompt = kt_all.reshape(depth, b, nh, HEAD_DIM, s).transpose(0, 1, 4, 2, 3)
    v_prompt = vt_all.reshape(depth, b, nh, HEAD_DIM, s).transpose(0, 1, 4, 2, 3)
    return (xp, xs, k_prompt, v_prompt, jnp.stack(cp), jnp.stack(ksm), jnp.stack(vsm), jnp.stack(csm))
```

```python
import functools

import jax
import jax.numpy as jnp
from jax import lax
from jax.experimental import pallas as pl
from jax.experimental.pallas import tpu as pltpu

F32 = jnp.float32
BF16 = jnp.bfloat16

LN_EPS = 1e-5
HEAD_DIM = 64
MOBA_BLOCK = 256
MOBA_TOPK = 3
CONV_WIDTH = 31
PAGE_SIZE = 128
N_MOD = 6
SH1, SC1, G1, SH2, SC2, G2 = range(N_MOD)

SUBLANES = 8
LANES = 128
CONV_HALO = 32
CONV_ROWS = 64
NEG_BIG = -1e30
LOG2_E = 1.4426950408889634
VMEM_LIMIT = 56 * 1024 * 1024

PROMPT_ROWS = 512
SAMPLE_SEQS = 32
KV_TILE_BLOCKS = 4
SAMPLE_ATTN_SEQS = 2
ATTN_HEAD_PAIRS = 1
ONES_ROWS = 16


def _cparams(*sem):
    return pltpu.CompilerParams(dimension_semantics=sem, vmem_limit_bytes=VMEM_LIMIT)


def _layer_norm(x, g, b):
    mu = jnp.mean(x, axis=-1, keepdims=True)
    xc = x - mu
    var = jnp.mean(xc * xc, axis=-1, keepdims=True)
    return xc * lax.rsqrt(var + LN_EPS) * g + b


def _rms_scale(x, g):
    return x * lax.rsqrt(jnp.mean(x * x, axis=-1, keepdims=True) + LN_EPS) * g


def _dot(a, b):
    return jnp.dot(a, b, preferred_element_type=F32)


def _dot_t(a, b, precision=None):
    return lax.dot_general(a, b, (((1,), (1,)), ((), ())), preferred_element_type=F32, precision=precision)


def _ln_kernel(x_ref, g_ref, b_ref, o_ref):
    o_ref[...] = _layer_norm(x_ref[...], g_ref[...], b_ref[...])


def _input_ln(x2, g, b, rows):
    n, d = x2.shape
    return pl.pallas_call(
        _ln_kernel,
        grid=(n // rows,),
        in_specs=[pl.BlockSpec((rows, d), lambda i: (i, 0)),
                  pl.BlockSpec((1, d), lambda i: (0, 0)),
                  pl.BlockSpec((1, d), lambda i: (0, 0))],
        out_specs=pl.BlockSpec((rows, d), lambda i: (i, 0)),
        out_shape=jax.ShapeDtypeStruct((n, d), F32),
        compiler_params=_cparams("parallel"),
        name="input_ln",
    )(x2, g.reshape(1, d), b.reshape(1, d))


def _ada_kernel(c_ref, w_ref, b_ref, o_ref):
    c = c_ref[...]
    h = (c * jax.nn.sigmoid(c)).astype(BF16)
    o_ref[0] = _dot(h, w_ref[0]) + b_ref[0]


def _ada(c_all, w_ada_bf, b_ada):
    depth, d, _ = w_ada_bf.shape
    r = c_all.shape[0]
    out = pl.pallas_call(
        _ada_kernel,
        grid=(depth, N_MOD),
        in_specs=[pl.BlockSpec((r, d), lambda l, j: (0, 0)),
                  pl.BlockSpec((1, d, d), lambda l, j: (l, 0, j)),
                  pl.BlockSpec((1, 1, d), lambda l, j: (l * N_MOD + j, 0, 0))],
        out_specs=pl.BlockSpec((1, r, d), lambda l, j: (l * N_MOD + j, 0, 0)),
        out_shape=jax.ShapeDtypeStruct((depth * N_MOD, r, d), F32),
        compiler_params=_cparams("parallel", "parallel"),
        name="ada_mod",
    )(c_all, w_ada_bf, b_ada.reshape(depth * N_MOD, 1, d))
    return out.reshape(depth * N_MOD, r, 1, d)


def _conv_post(y, clg, clb, ogc):
    yn = _layer_norm(y, clg, clb)
    ys = yn * jax.nn.sigmoid(yn)
    return _rms_scale(ys, ogc).astype(BF16)


def _inproj_prompt_kernel(x_ref, sc_ref, sh_ref, w_ref, wt_ref, wdw_ref, bdw_ref, clg_ref, clb_ref, ogc_ref,
                          yc_ref, qt_ref, kt_ref, vt_ref, kb_ref, vbt_ref, km_ref, cn_ref, ubuf, ushift,
                          *, tm, dc, da):
    i = pl.program_id(1)

    @pl.when(i == 0)
    def _():
        ubuf[0:CONV_HALO, :] = jnp.zeros((CONV_HALO, dc), F32)

    x = x_ref[0]
    h = (x * (1.0 + sc_ref[0, 0]) + sh_ref[0, 0]).astype(BF16)

    zc = _dot(h, w_ref[0, :, 0:2 * dc])
    ubuf[CONV_HALO:CONV_HALO + tm, :] = zc[:, :dc] * jax.nn.sigmoid(zc[:, dc:])

    first = CONV_HALO - (CONV_WIDTH - 1)
    span = tm + CONV_HALO - SUBLANES
    for r in range(1, SUBLANES):
        ushift[r - 1] = ubuf[r:r + span, :]
    for c in range(tm // CONV_ROWS):
        groups = []
        for g in range(dc // LANES):
            ls = slice(g * LANES, (g + 1) * LANES)
            acc = jnp.broadcast_to(bdw_ref[0, :, ls], (CONV_ROWS, LANES))
            for j in range(CONV_WIDTH):
                a, r = divmod(first + j, SUBLANES)
                r0 = c * CONV_ROWS + a * SUBLANES
                src = ubuf[r0:r0 + CONV_ROWS, ls] if r == 0 else ushift[r - 1, r0:r0 + CONV_ROWS, ls]
                acc = acc + wdw_ref[0, j:j + 1, ls] * src
            groups.append(acc)
        y = jnp.concatenate(groups, axis=1)
        yc_ref[c * CONV_ROWS:(c + 1) * CONV_ROWS, :] = _conv_post(y, clg_ref[0], clb_ref[0], ogc_ref[0])

    tail = ubuf[tm:tm + CONV_HALO, :]
    cn_ref[0] = tail
    ubuf[0:CONV_HALO, :] = tail

    qt_ref[...] = _dot_t(wt_ref[0, 0:da, :], h) * (HEAD_DIM ** -0.5 * LOG2_E)
    kt_ref[...] = _dot_t(wt_ref[0, da:2 * da, :], h)
    zvt = _dot_t(wt_ref[0, 2 * da:3 * da, :], h)
    vt_ref[...] = zvt
    vbt_ref[...] = zvt.astype(BF16)
    c0 = 2 * dc
    zk = _dot(h, w_ref[0, :, c0 + da:c0 + 2 * da])
    kb_ref[...] = zk.astype(BF16)
    for r in range(tm // MOBA_BLOCK):
        km_ref[0, r:r + 1, :] = jnp.mean(zk[r * MOBA_BLOCK:(r + 1) * MOBA_BLOCK], axis=0, keepdims=True)


def _inproj_prompt_kernel_kv(*refs, n_in, **kw):
    _inproj_prompt_kernel(*refs[:n_in], *refs[n_in + 2:], **kw)


def _inproj_prompt(l, depth, x3, mod, w_in_bf, w_qkv_t, w_dw, b_dw3, clg3, clb3, ogc3, tm, kv_all):
    b, s, d = x3.shape
    dc = w_dw.shape[-1]
    da = (w_in_bf.shape[-1] - 2 * dc) // 3
    r = mod.shape[1]
    nt = s // tm
    nbt = tm // MOBA_BLOCK
    n = b * s
    prow = r - SUBLANES
    kw = dict(tm=tm, dc=dc, da=da)

    def vec(a):
        return pl.BlockSpec((1, 1, a.shape[-1]), lambda bi, i: (l, 0, 0))

    def rows(width):
        return pl.BlockSpec((tm, width), lambda bi, i: (bi * nt + i, 0))

    cols = pl.BlockSpec((da, tm), lambda bi, i: (bi, i))
    cols_l = pl.BlockSpec((da, tm), lambda bi, i: (l * b + bi, i))

    in_specs = [pl.BlockSpec((1, tm, d), lambda bi, i: (bi, i, 0)),
                pl.BlockSpec((1, 1, 1, d), lambda bi, i: (l * N_MOD + SC1, prow + bi, 0, 0)),
                pl.BlockSpec((1, 1, 1, d), lambda bi, i: (l * N_MOD + SH1, prow + bi, 0, 0)),
                pl.BlockSpec((1, d, w_in_bf.shape[-1]), lambda bi, i: (l, 0, 0)),
                pl.BlockSpec((1, 3 * da, d), lambda bi, i: (l, 0, 0)),
                pl.BlockSpec((1, CONV_WIDTH, dc), lambda bi, i: (l, 0, 0)),
                vec(b_dw3), vec(clg3), vec(clb3), vec(ogc3)]
    args = [x3, mod, mod, w_in_bf, w_qkv_t, w_dw, b_dw3, clg3, clb3, ogc3]
    n_in = len(args)
    if kv_all is None:
        kern, aliases = functools.partial(_inproj_prompt_kernel, **kw), {}
    else:
        kern = functools.partial(_inproj_prompt_kernel_kv, n_in=n_in, **kw)
        in_specs += [pl.BlockSpec(memory_space=pl.ANY)] * 2
        args += list(kv_all)
        aliases = {n_in: 2, n_in + 1: 3}

    return pl.pallas_call(
        kern,
        grid=(b, nt),
        in_specs=in_specs,
        out_specs=[rows(dc), cols, cols_l, cols_l, rows(da), cols,
                   pl.BlockSpec((1, nbt, da), lambda bi, i: (bi * nt + i, 0, 0)),
                   pl.BlockSpec((1, CONV_HALO, dc), lambda bi, i: (bi, 0, 0))],
        out_shape=[jax.ShapeDtypeStruct((n, dc), BF16),
                   jax.ShapeDtypeStruct((b * da, s), F32),
                   jax.ShapeDtypeStruct((depth * b * da, s), F32),
                   jax.ShapeDtypeStruct((depth * b * da, s), F32),
                   jax.ShapeDtypeStruct((n, da), BF16),
                   jax.ShapeDtypeStruct((b * da, s), BF16),
                   jax.ShapeDtypeStruct((b * nt, nbt, da), F32),
                   jax.ShapeDtypeStruct((b, CONV_HALO, dc), F32)],
        scratch_shapes=[pltpu.VMEM((CONV_HALO + tm, dc), F32),
                        pltpu.VMEM((SUBLANES - 1, tm + CONV_HALO - SUBLANES, dc), F32)],
        input_output_aliases=aliases,
        compiler_params=_cparams("parallel", "arbitrary"),
        name=f"inproj_prompt_{l}",
    )(*args)


def _inproj_sample_kernel(x_ref, sc_ref, sh_ref, st_ref, w_ref, wdw_ref, bdw_ref, clg_ref, clb_ref, ogc_ref,
                          yc_ref, q_ref, k_ref, v_ref, cn_ref, uext, *, tb, ln, dc, da):
    d = x_ref.shape[-1]
    h = (x_ref[...] * (1.0 + sc_ref[0]) + sh_ref[0]).reshape(tb * ln, d).astype(BF16)

    zc = _dot(h, w_ref[0, :, 0:2 * dc])
    u = zc[:, :dc] * jax.nn.sigmoid(zc[:, dc:])
    uext[:, 0:CONV_HALO, :] = st_ref[...]
    uext[:, CONV_HALO:CONV_HALO + ln, :] = u.reshape(tb, ln, dc)

    first = CONV_HALO - (CONV_WIDTH - 1)
    acc = jnp.broadcast_to(bdw_ref[0], (tb, ln, dc))
    for j in range(CONV_WIDTH):
        acc = acc + wdw_ref[0, j:j + 1, :] * uext[:, first + j:first + j + ln, :]
    yc_ref[...] = _conv_post(acc.reshape(tb * ln, dc), clg_ref[0], clb_ref[0], ogc_ref[0])
    cn_ref[...] = uext[:, ln:ln + CONV_HALO, :]

    c0 = 2 * dc
    q_ref[...] = _dot(h, w_ref[0, :, c0:c0 + da]) * (HEAD_DIM ** -0.5)
    k_ref[...] = _dot(h, w_ref[0, :, c0 + da:c0 + 2 * da])
    v_ref[...] = _dot(h, w_ref[0, :, c0 + 2 * da:c0 + 3 * da])


def _inproj_sample(l, x3, mod, state_pad, w_in_bf, w_dw, b_dw3, clg3, clb3, ogc3, tb):
    bd, ln, d = x3.shape
    dc = w_dw.shape[-1]
    da = (w_in_bf.shape[-1] - 2 * dc) // 3
    n = bd * ln
    kern = functools.partial(_inproj_sample_kernel, tb=tb, ln=ln, dc=dc, da=da)

    def vec(a):
        return pl.BlockSpec((1, 1, a.shape[-1]), lambda i: (l, 0, 0))

    def rows(width):
        return pl.BlockSpec((tb * ln, width), lambda i: (i, 0))

    return pl.pallas_call(
        kern,
        grid=(bd // tb,),
        in_specs=[pl.BlockSpec((tb, ln, d), lambda i: (i, 0, 0)),
                  pl.BlockSpec((1, tb, 1, d), lambda i: (l * N_MOD + SC1, i, 0, 0)),
                  pl.BlockSpec((1, tb, 1, d), lambda i: (l * N_MOD + SH1, i, 0, 0)),
                  pl.BlockSpec((None, tb, CONV_HALO, dc), lambda i: (l, i, 0, 0)),
                  pl.BlockSpec((1, d, w_in_bf.shape[-1]), lambda i: (l, 0, 0)),
                  pl.BlockSpec((1, CONV_WIDTH, dc), lambda i: (l, 0, 0)),
                  vec(b_dw3), vec(clg3), vec(clb3), vec(ogc3)],
        out_specs=[rows(dc), rows(da), rows(da), rows(da),
                   pl.BlockSpec((tb, CONV_HALO, dc), lambda i: (i, 0, 0))],
        out_shape=[jax.ShapeDtypeStruct((n, dc), BF16),
                   jax.ShapeDtypeStruct((n, da), F32),
                   jax.ShapeDtypeStruct((n, da), F32),
                   jax.ShapeDtypeStruct((n, da), F32),
                   jax.ShapeDtypeStruct((bd, CONV_HALO, dc), F32)],
        scratch_shapes=[pltpu.VMEM((tb, CONV_HALO + ln, dc), F32)],
        compiler_params=_cparams("parallel"),
        name=f"inproj_sample_{l}",
    )(x3, mod, mod, state_pad, w_in_bf, w_dw, b_dw3, clg3, clb3, ogc3)


def _topk_mask(gate, valid, pos, k, axis):
    g = jnp.where(valid, gate, -jnp.inf)
    sel = jnp.zeros(gate.shape, jnp.bool_)
    for _ in range(k):
        mx = jnp.max(g, axis=axis, keepdims=True)
        idx = jnp.min(jnp.where(g == mx, pos, float(gate.shape[axis])), axis=axis, keepdims=True)
        pick = (pos == idx) & (mx > -jnp.inf)
        sel = sel | pick
        g = jnp.where(pick, -jnp.inf, g)
    return sel


def _attn_prompt_kernel(qt_ref, kb_ref, vbt_ref, km_ref, o_ref, kaug, vaug, qaug, s_own, s_a, s_b, m_sc, acc_sc,
                        *, nb, kvb):
    tq = MOBA_BLOCK
    kc = kvb * MOBA_BLOCK
    lane = lax.broadcasted_iota(jnp.int32, (tq, LANES), 1)
    head_lanes = (lane < HEAD_DIM, lane >= HEAD_DIM)
    pen_base = (HEAD_DIM, 0)

    nhs = kaug.shape[0]
    for n in range(nb):
        for hh in range(nhs):
            pp, hi = divmod(hh, 2)
            kblk = kb_ref[n * tq:(n + 1) * tq, pp * LANES:(pp + 1) * LANES]
            onehot = jnp.where(lane == pen_base[hi] + n, 1.0, 0.0).astype(BF16)
            kaug[hh, n * tq:(n + 1) * tq, :] = jnp.where(head_lanes[hi], kblk, onehot)

    for hh in range(nhs):
        vaug[hh, 0:HEAD_DIM, :] = vbt_ref[hh * HEAD_DIM:(hh + 1) * HEAD_DIM, :]
        vaug[hh, HEAD_DIM:, :] = jnp.ones((ONES_ROWS, vaug.shape[-1]), BF16)

    lane_nb = lax.broadcasted_iota(jnp.int32, (nb, LANES), 1)
    km2 = []
    for pp in range(nhs // 2):
        km = km_ref[0, :, pp * LANES:(pp + 1) * LANES]
        km2.append(jnp.concatenate([jnp.where(lane_nb < HEAD_DIM, km, 0.0),
                                    jnp.where(lane_nb >= HEAD_DIM, km, 0.0)], axis=0))

    def scores(j, c, buf, qa, causal):
        k0 = pl.multiple_of(c * kc, kc)
        for hh in range(nhs):
            for sb in range(kvb):
                ks = k0 + sb * tq
                buf[hh, sb * tq:(sb + 1) * tq, :] = _dot(kaug[hh, pl.ds(ks, tq), :], qa[hh])
        if causal:
            own = pl.multiple_of((j % kvb) * tq, tq)
            key_i = lax.broadcasted_iota(jnp.int32, (tq, tq), 0)
            qry_i = lax.broadcasted_iota(jnp.int32, (tq, tq), 1)
            for hh in range(nhs):
                buf[hh, pl.ds(own, tq), :] = jnp.where(key_i <= qry_i, buf[hh, pl.ds(own, tq), :], NEG_BIG)

    def softmax_pv(c, buf):
        k0 = pl.multiple_of(c * kc, kc)
        m_hd = []
        for hh in range(nhs):
            m_cur = jnp.max(buf[hh, 0:tq, :], axis=0, keepdims=True)
            for sb in range(1, kvb):
                m_cur = jnp.maximum(m_cur, jnp.max(buf[hh, sb * tq:(sb + 1) * tq, :], axis=0, keepdims=True))
            m_hd.append(jnp.maximum(m_sc[hh], m_cur))
        for hh in range(nhs):
            m_row = m_hd[hh][0:1]
            alpha = jnp.exp2(m_sc[hh] - m_hd[hh])
            p_all = jnp.concatenate([jnp.exp2((buf[hh, sb * tq:(sb + 1) * tq, :] - m_row).astype(BF16))
                                     for sb in range(kvb)], axis=0)
            pv = _dot(vaug[hh, :, pl.ds(k0, kc)], p_all)
            acc_sc[hh] = alpha[0:1] * acc_sc[hh] + pv
            m_sc[hh] = m_hd[hh]

    def open_block(j, slot):
        j0 = pl.multiple_of(j * tq, tq)
        blk_i = lax.broadcasted_iota(jnp.int32, (nb, tq), 0)
        blk_f = blk_i.astype(F32)
        zfill = jnp.zeros((HEAD_DIM - nb, tq), F32)
        for pp in range(nhs // 2):
            qt = qt_ref[pp * LANES:(pp + 1) * LANES, pl.ds(j0, tq)]
            gate2 = jnp.dot(km2[pp], qt, preferred_element_type=F32, precision=lax.Precision.HIGHEST)
            for hi in range(2):
                sel = _topk_mask(gate2[hi * nb:(hi + 1) * nb], blk_i < j, blk_f, min(MOBA_TOPK, nb), 0)
                pen_t = jnp.where(sel | (blk_i == j), 0.0, NEG_BIG)
                if hi == 0:
                    rows_t = [qt[0:HEAD_DIM], pen_t, zfill]
                else:
                    rows_t = [pen_t, zfill, qt[HEAD_DIM:2 * HEAD_DIM]]
                qaug[slot, 2 * pp + hi] = jnp.concatenate(rows_t, axis=0).astype(BF16)
        scores(j, j // kvb, s_own.at[slot], qaug.at[slot], True)

    n_steps = nb // kvb

    def query_block(j, carry):
        slot = j % 2
        m_sc[...] = jnp.full(m_sc.shape, -jnp.inf, F32)
        acc_sc[...] = jnp.zeros(acc_sc.shape, F32)
        c_own = j // kvb

        def tile_buf(i):
            if i == 0:
                return s_own.at[slot]
            return s_b if i % 2 else s_a

        for i in range(n_steps):
            @pl.when(i <= c_own)
            def _():
                if i + 1 < n_steps:
                    scores(j, jnp.maximum(c_own - i - 1, 0), tile_buf(i + 1), qaug.at[slot], False)
                softmax_pv(c_own - i, tile_buf(i))
                if i == 0:
                    open_block(jnp.minimum(j + 1, nb - 1), 1 - slot)

        j0 = pl.multiple_of(j * tq, tq)
        o_t = jnp.concatenate([acc_sc[hh, 0:HEAD_DIM] / acc_sc[hh, HEAD_DIM:HEAD_DIM + 1] for hh in range(nhs)],
                              axis=0)
        o_ref[pl.ds(j0, tq), :] = o_t.T
        return carry

    open_block(0, 0)
    lax.fori_loop(0, nb, query_block, 0)


def _attn_prompt(l, qt2, kb2, vbt2, km3, b, s):
    n, da = kb2.shape
    nb = s // MOBA_BLOCK
    hp = da // LANES
    tq = MOBA_BLOCK
    kvb = max(c for c in range(1, KV_TILE_BLOCKS + 1) if nb % c == 0)
    km3 = km3.reshape(b, nb, da)
    npp = ATTN_HEAD_PAIRS if hp % ATTN_HEAD_PAIRS == 0 else 1
    wid, nhs, steps = npp * LANES, 2 * npp, hp // npp
    kern = functools.partial(_attn_prompt_kernel, nb=nb, kvb=kvb)
    return pl.pallas_call(
        kern,
        grid=(b, steps),
        in_specs=[pl.BlockSpec((wid, s), lambda bi, p: (bi * steps + p, 0)),
                  pl.BlockSpec((s, wid), lambda bi, p: (bi, p)),
                  pl.BlockSpec((wid, s), lambda bi, p: (bi * steps + p, 0)),
                  pl.BlockSpec((1, nb, wid), lambda bi, p: (bi, 0, p))],
        out_specs=pl.BlockSpec((s, wid), lambda bi, p: (bi, p)),
        out_shape=jax.ShapeDtypeStruct((n, da), F32),
        scratch_shapes=[pltpu.VMEM((nhs, s, LANES), BF16),
                        pltpu.VMEM((nhs, HEAD_DIM + ONES_ROWS, s), BF16),
                        pltpu.VMEM((2, nhs, LANES, tq), BF16),
                        pltpu.VMEM((2, nhs, kvb * tq, tq), F32),
                        pltpu.VMEM((nhs, kvb * tq, tq), F32),
                        pltpu.VMEM((nhs, kvb * tq, tq), F32),
                        pltpu.VMEM((nhs, SUBLANES, tq), F32),
                        pltpu.VMEM((nhs, HEAD_DIM + ONES_ROWS, tq), F32)],
        compiler_params=_cparams("parallel", "parallel"),
        name=f"attn_prompt_{l}",
    )(qt2, kb2, vbt2, km3)


def _attn_sample_kernel(pt_ref, q_ref, kn_ref, vn_ref, *refs, n_pages, nh, ln, spq):
    del pt_ref
    o_ref, kbuf, vbuf = refs[2 * spq * n_pages:]
    for sq in range(spq):
        k_pages = refs[sq * n_pages:(sq + 1) * n_pages]
        v_pages = refs[(spq + sq) * n_pages:(spq + sq + 1) * n_pages]
        o_ref[sq] = _attn_sample_one(q_ref[sq], kn_ref[sq], vn_ref[sq], k_pages, v_pages, kbuf.at[sq], vbuf.at[sq],
                                     nh=nh, ln=ln)


def _attn_sample_one(q, kn_new, vn_new, k_pages, v_pages, kbuf, vbuf, *, nh, ln):
    n_pages = len(k_pages)
    da = nh * HEAD_DIM
    rows = nh * ln
    ppb = MOBA_BLOCK // PAGE_SIZE
    nbp = n_pages // ppb

    qt = jnp.concatenate([q] * nh, axis=0)
    row = lax.broadcasted_iota(jnp.int32, (rows, da), 0)
    lane = lax.broadcasted_iota(jnp.int32, (rows, da), 1)
    own_head = (lane // HEAD_DIM) == (row // ln)
    qbd = jnp.where(own_head, qt, 0.0)
    qbd_bf = qbd.astype(BF16)

    km_lane = lax.broadcasted_iota(jnp.int32, (da, LANES), 1)
    km_t = jnp.zeros((da, LANES), F32)
    for n in range(nbp):
        ksum = jnp.zeros((da, PAGE_SIZE), F32)
        for pg in range(ppb):
            p_i = n * ppb + pg
            kp = k_pages[p_i][0]
            ksum = ksum + kp
            kbuf[:, p_i * PAGE_SIZE:(p_i + 1) * PAGE_SIZE] = kp.astype(BF16)
            vbuf[:, p_i * PAGE_SIZE:(p_i + 1) * PAGE_SIZE] = v_pages[p_i][0].astype(BF16)
        kmean = jnp.sum(ksum, axis=1, keepdims=True) * (1.0 / MOBA_BLOCK)
        km_t = jnp.where(km_lane == n, kmean, km_t)

    gate = jnp.dot(qbd, km_t, preferred_element_type=F32, precision=lax.Precision.HIGHEST)
    glane = lax.broadcasted_iota(jnp.int32, (rows, LANES), 1)
    sel = _topk_mask(gate, glane < nbp, glane.astype(F32), min(MOBA_TOPK, nbp), 1)
    bias = jnp.where(sel, 0.0, NEG_BIG)

    zrows = jnp.zeros((ln, da), F32)
    kn = jnp.concatenate([kn_new, zrows], axis=0).astype(BF16)
    vn = jnp.concatenate([vn_new, zrows], axis=0).astype(BF16)
    s_own = _dot_t(qbd_bf, kn)
    orow = lax.broadcasted_iota(jnp.int32, (rows, 2 * ln), 0)
    ocol = lax.broadcasted_iota(jnp.int32, (rows, 2 * ln), 1)
    s_own = jnp.where(ocol <= orow % ln, s_own, NEG_BIG)
    m = jnp.max(s_own, axis=-1, keepdims=True)

    s_all = _dot(qbd_bf, kbuf[...])
    s_past = []
    for n in range(nbp):
        s = s_all[:, n * MOBA_BLOCK:(n + 1) * MOBA_BLOCK] + bias[:, n:n + 1]
        s_past.append(s)
        m = jnp.maximum(m, jnp.max(s, axis=-1, keepdims=True))

    p_own = jnp.exp(s_own - m)
    lsum = jnp.sum(p_own, axis=-1, keepdims=True)
    p_past = []
    for n in range(nbp):
        p = jnp.exp(s_past[n] - m)
        lsum = lsum + jnp.sum(p, axis=-1, keepdims=True)
        p_past.append(p.astype(BF16))
    acc = _dot(p_own.astype(BF16), vn) + _dot_t(jnp.concatenate(p_past, axis=1), vbuf[...])

    o = jnp.where(own_head, acc / lsum, 0.0)
    out = o[0:ln]
    for hh in range(1, nh):
        out = out + o[hh * ln:(hh + 1) * ln]
    return out


def _attn_sample(l, q3, k3, v3, ck_t, cv_t, pt_flat, n_pool, n_pages):
    bd, ln, da = q3.shape
    nh = da // HEAD_DIM
    spq = SAMPLE_ATTN_SEQS if bd % SAMPLE_ATTN_SEQS == 0 else 1
    kern = functools.partial(_attn_sample_kernel, n_pages=n_pages, nh=nh, ln=ln, spq=spq)

    def page_spec(sq, p_i):
        return pl.BlockSpec((1, da, PAGE_SIZE),
                            lambda i, pt: (l * n_pool + pt[(i * spq + sq) * n_pages + p_i], 0, 0))

    pages = [page_spec(sq, p_i) for sq in range(spq) for p_i in range(n_pages)]
    tok = pl.BlockSpec((spq, ln, da), lambda i, pt: (i, 0, 0))
    grid_spec = pltpu.PrefetchScalarGridSpec(
        num_scalar_prefetch=1,
        grid=(bd // spq,),
        in_specs=[tok, tok, tok] + pages * 2,
        out_specs=tok,
        scratch_shapes=[pltpu.VMEM((spq, da, n_pages * PAGE_SIZE), BF16),
                        pltpu.VMEM((spq, da, n_pages * PAGE_SIZE), BF16)],
    )
    return pl.pallas_call(
        kern,
        grid_spec=grid_spec,
        out_shape=jax.ShapeDtypeStruct((bd, ln, da), F32),
        compiler_params=_cparams("arbitrary"),
        name=f"attn_sample_{l}",
    )(pt_flat, q3, k3, v3, *([ck_t] * (spq * n_pages)), *([cv_t] * (spq * n_pages)))


def _outmlp_kernel(x_ref, yc_ref, ya_ref, g1_ref, sc2_ref, sh2_ref, g2_ref, oga_ref, wout_ref,
                   ln1g_ref, ln1b_ref, w1_ref, b1_ref, w2_ref, b2_ref, ln2g_ref, ln2b_ref, o_ref,
                   *, alpha, ff_chunk):
    x = x_ref[...]
    blk = x.shape
    rows, d = blk[0] * blk[1], blk[2]
    dc = yc_ref.shape[-1]
    dff = w1_ref.shape[-1]

    yan = _rms_scale(ya_ref[...], oga_ref[0]).astype(BF16)
    mix = _dot(yc_ref[...], wout_ref[0, 0:dc, :]) + _dot(yan, wout_ref[0, dc:, :])
    x1 = _layer_norm(alpha * x + (1.0 + g1_ref[0]) * mix.reshape(blk), ln1g_ref[0], ln1b_ref[0])

    h2 = (x1 * (1.0 + sc2_ref[0]) + sh2_ref[0]).reshape(rows, d).astype(BF16)
    f = jnp.broadcast_to(b2_ref[0], (rows, d))
    for c in range(dff // ff_chunk):
        cs = slice(c * ff_chunk, (c + 1) * ff_chunk)
        hid = jnp.maximum(_dot(h2, w1_ref[0, :, cs]) + b1_ref[0, :, cs], 0.0)
        f = f + _dot((hid * hid).astype(BF16), w2_ref[0, cs, :])
    o_ref[...] = _layer_norm(alpha * x1 + (1.0 + g2_ref[0]) * f.reshape(blk), ln2g_ref[0], ln2b_ref[0])


def _outmlp(l, x3, yc2, ya2, mod, mod_row0, per_row_mod, oga3, w_out_bf, ln1g3, ln1b3, w1_bf, b13, w2_bf,
            b23, ln2g3, ln2b3, alpha, grp, tag):
    a, r, d = x3.shape
    dc, da = yc2.shape[-1], ya2.shape[-1]
    dff = w1_bf.shape[-1]
    if per_row_mod:
        blk = (grp, r, d)
        grid = (a // grp,)
        xmap = lambda i: (i, 0, 0)
        rmap = lambda i: (i, 0)
        brows = grp * r
        mblk = (1, grp, 1, d)

        def mod_spec(comp):
            return pl.BlockSpec(mblk, lambda i: (l * N_MOD + comp, i, 0, 0))
        sem = ("parallel",)
    else:
        nt = r // grp
        blk = (1, grp, d)
        grid = (a, nt)
        xmap = lambda bi, i: (bi, i, 0)
        rmap = lambda bi, i: (bi * nt + i, 0)
        brows = grp
        mblk = (1, 1, 1, d)

        def mod_spec(comp):
            return pl.BlockSpec(mblk, lambda bi, i: (l * N_MOD + comp, mod_row0 + bi, 0, 0))
        sem = ("parallel", "parallel")

    nargs = len(grid)

    def const(shape):
        zeros = (0,) * (len(shape) - 1)
        if nargs == 1:
            return pl.BlockSpec(shape, lambda i: (l,) + zeros, pipeline_mode=pl.Buffered(1))
        return pl.BlockSpec(shape, lambda bi, i: (l,) + zeros, pipeline_mode=pl.Buffered(1))

    kern = functools.partial(_outmlp_kernel, alpha=alpha, ff_chunk=min(dff, 1024))
    return pl.pallas_call(
        kern,
        grid=grid,
        in_specs=[pl.BlockSpec(blk, xmap),
                  pl.BlockSpec((brows, dc), rmap),
                  pl.BlockSpec((brows, da), rmap),
                  mod_spec(G1), mod_spec(SC2), mod_spec(SH2), mod_spec(G2),
                  const((1, 1, da)), const((1, dc + da, d)),
                  const((1, 1, d)), const((1, 1, d)),
                  const((1, d, dff)), const((1, 1, dff)), const((1, dff, d)), const((1, 1, d)),
                  const((1, 1, d)), const((1, 1, d))],
        out_specs=pl.BlockSpec(blk, xmap),
        out_shape=jax.ShapeDtypeStruct((a, r, d), F32),
        compiler_params=_cparams(*sem),
        name=f"outmlp_{tag}_{l}",
    )(x3, yc2, ya2, mod, mod, mod, mod, oga3, w_out_bf, ln1g3, ln1b3, w1_bf, b13, w2_bf, b23, ln2g3, ln2b3)


def kernel(x_prompt, x_sample, cache_k, cache_v, state_conv, page_table, c_prompt, c_sample, ln0_g, ln0_b, w_ada, b_ada, w_in, w_dw, b_dw, conv_ln_g, conv_ln_b, out_g_conv, out_g_attn, w_out, ln1_g, ln1_b, w1, b1, w2, b2, ln2_g, ln2_b):
    b, s, d = x_prompt.shape
    bd, ln, _ = x_sample.shape
    depth = w_in.shape[0]
    dc = w_dw.shape[-1]
    da = out_g_attn.shape[-1]
    nh = da // HEAD_DIM
    n_pool = cache_k.shape[1]
    n_pages = page_table.shape[1]
    alpha = (2 * depth) ** 0.25
    assert (n_pages * PAGE_SIZE) % MOBA_BLOCK == 0 and ln <= MOBA_BLOCK and ln == SUBLANES
    assert s % MOBA_BLOCK == 0 and s // MOBA_BLOCK <= HEAD_DIM and da % LANES == 0
    tm = min(PROMPT_ROWS, s)
    tb = min(SAMPLE_SEQS, bd)

    w_ada_bf, w_in_bf, w_out_bf = w_ada.astype(BF16), w_in.astype(BF16), w_out.astype(BF16)
    w1_bf, w2_bf = w1.astype(BF16), w2.astype(BF16)
    w_qkv_t = w_in[:, :, 2 * dc:].transpose(0, 2, 1).astype(BF16)

    def vec3(a):
        return a.reshape(depth, 1, a.shape[-1])

    b_dw3, clg3, clb3, ogc3, oga3 = vec3(b_dw), vec3(conv_ln_g), vec3(conv_ln_b), vec3(out_g_conv), vec3(out_g_attn)
    ln1g3, ln1b3, ln2g3, ln2b3, b13, b23 = vec3(ln1_g), vec3(ln1_b), vec3(ln2_g), vec3(ln2_b), vec3(b1), vec3(b2)

    n_rows = bd + SUBLANES
    assert b <= SUBLANES and bd % SUBLANES == 0
    c_all = jnp.concatenate([c_sample, c_prompt, jnp.zeros((SUBLANES - b, d), F32)], axis=0)
    mod = _ada(c_all, w_ada_bf, b_ada)
    mod_row0 = n_rows - SUBLANES

    xp = _input_ln(x_prompt.reshape(b * s, d), ln0_g, ln0_b, tm).reshape(b, s, d)
    xs = _input_ln(x_sample.reshape(bd * ln, d), ln0_g, ln0_b, min(bd * ln, 512)).reshape(bd, ln, d)

    state_pad = jnp.pad(state_conv, ((0, 0), (0, 0), (CONV_HALO - (CONV_WIDTH - 1), 0), (0, 0)))
    ck2 = cache_k.transpose(0, 1, 3, 4, 2).reshape(depth * n_pool, da, PAGE_SIZE)
    cv2 = cache_v.transpose(0, 1, 3, 4, 2).reshape(depth * n_pool, da, PAGE_SIZE)
    pt_flat = page_table.reshape(-1).astype(jnp.int32)

    cp, ksm, vsm, csm = [], [], [], []
    kt_all = vt_all = None
    for l in range(depth):
        yc, qt2, kt_all, vt_all, kb2, vbt2, km3, cn = _inproj_prompt(
            l, depth, xp, mod, w_in_bf, w_qkv_t, w_dw, b_dw3, clg3, clb3, ogc3, tm,
            None if l == 0 else (kt_all, vt_all))
        ya = _attn_prompt(l, qt2, kb2, vbt2, km3, b, s)
        xp = _outmlp(l, xp, yc, ya, mod, mod_row0, False, oga3, w_out_bf, ln1g3, ln1b3, w1_bf, b13, w2_bf, b23,
                     ln2g3, ln2b3, alpha, tm, "prompt")
        cp.append(cn[:, CONV_HALO - (CONV_WIDTH - 1):])

        ycs, qs, ks, vs, cns = _inproj_sample(l, xs, mod, state_pad, w_in_bf, w_dw, b_dw3, clg3, clb3, ogc3, tb)
        yas = _attn_sample(l, qs.reshape(bd, ln, da), ks.reshape(bd, ln, da), vs.reshape(bd, ln, da),
                           ck2, cv2, pt_flat, n_pool, n_pages)
        xs = _outmlp(l, xs, ycs, yas.reshape(bd * ln, da), mod, 0, True, oga3, w_out_bf, ln1g3, ln1b3, w1_bf,
                     b13, w2_bf, b23, ln2g3, ln2b3, alpha, tb, "sample")
        ksm.append(ks.reshape(bd, ln, nh, HEAD_DIM))
        vsm.append(vs.reshape(bd, ln, nh, HEAD_DIM))
        csm.append(cns[:, CONV_HALO - (CONV_WIDTH - 1):])

    k_prompt = kt_all.reshape(depth, b, nh, HEAD_DIM, s).transpose(0, 1, 4, 2, 3)
    v_prompt = vt_all.reshape(depth, b, nh, HEAD_DIM, s).transpose(0, 1, 4, 2, 3)
    return (xp, xs, k_prompt, v_prompt, jnp.stack(cp), jnp.stack(ksm), jnp.stack(vsm), jnp.stack(csm))
```

```python
import functools

import jax
import jax.numpy as jnp
from jax import lax
from jax.experimental import pallas as pl
from jax.experimental.pallas import tpu as pltpu

F32 = jnp.float32
BF16 = jnp.bfloat16

LN_EPS = 1e-5
HEAD_DIM = 64
MOBA_BLOCK = 256
MOBA_TOPK = 3
CONV_WIDTH = 31
PAGE_SIZE = 128
N_MOD = 6
SH1, SC1, G1, SH2, SC2, G2 = range(N_MOD)

SUBLANES = 8
LANES = 128
CONV_HALO = 32
CONV_ROWS = 64
NEG_BIG = -1e30
LOG2_E = 1.4426950408889634
VMEM_LIMIT = 56 * 1024 * 1024

PROMPT_ROWS = 512
SAMPLE_SEQS = 32
KV_TILE_BLOCKS = 2
SAMPLE_ATTN_SEQS = 2
ATTN_HEAD_PAIRS = 1
ONES_ROWS = 16


def _cparams(*sem):
    return pltpu.CompilerParams(dimension_semantics=sem, vmem_limit_bytes=VMEM_LIMIT)


def _layer_norm(x, g, b):
    mu = jnp.mean(x, axis=-1, keepdims=True)
    xc = x - mu
    var = jnp.mean(xc * xc, axis=-1, keepdims=True)
    return xc * lax.rsqrt(var + LN_EPS) * g + b


def _rms_scale(x, g):
    return x * lax.rsqrt(jnp.mean(x * x, axis=-1, keepdims=True) + LN_EPS) * g


def _dot(a, b):
    return jnp.dot(a, b, preferred_element_type=F32)


def _split_bf16(x):
    hi = x.astype(BF16)
    return hi, (x - hi.astype(F32)).astype(BF16)


def _dot_3pass(a, b):
    a_hi, a_lo = _split_bf16(a)
    b_hi, b_lo = _split_bf16(b)
    return _dot(a_hi, b_hi) + (_dot(a_hi, b_lo) + _dot(a_lo, b_hi))


def _dot_t(a, b, precision=None):
    return lax.dot_general(a, b, (((1,), (1,)), ((), ())), preferred_element_type=F32, precision=precision)


def _ln_kernel(x_ref, g_ref, b_ref, o_ref):
    o_ref[...] = _layer_norm(x_ref[...], g_ref[...], b_ref[...])


def _input_ln(x2, g, b, rows):
    n, d = x2.shape
    return pl.pallas_call(
        _ln_kernel,
        grid=(n // rows,),
        in_specs=[pl.BlockSpec((rows, d), lambda i: (i, 0)),
                  pl.BlockSpec((1, d), lambda i: (0, 0)),
                  pl.BlockSpec((1, d), lambda i: (0, 0))],
        out_specs=pl.BlockSpec((rows, d), lambda i: (i, 0)),
        out_shape=jax.ShapeDtypeStruct((n, d), F32),
        compiler_params=_cparams("parallel"),
        name="input_ln",
    )(x2, g.reshape(1, d), b.reshape(1, d))


def _ada_kernel(c_ref, w_ref, b_ref, o_ref):
    c = c_ref[...]
    h = (c * jax.nn.sigmoid(c)).astype(BF16)
    o_ref[0] = _dot(h, w_ref[0]) + b_ref[0]


def _ada(c_all, w_ada_bf, b_ada):
    depth, d, _ = w_ada_bf.shape
    r = c_all.shape[0]
    out = pl.pallas_call(
        _ada_kernel,
        grid=(depth, N_MOD),
        in_specs=[pl.BlockSpec((r, d), lambda l, j: (0, 0)),
                  pl.BlockSpec((1, d, d), lambda l, j: (l, 0, j)),
                  pl.BlockSpec((1, 1, d), lambda l, j: (l * N_MOD + j, 0, 0))],
        out_specs=pl.BlockSpec((1, r, d), lambda l, j: (l * N_MOD + j, 0, 0)),
        out_shape=jax.ShapeDtypeStruct((depth * N_MOD, r, d), F32),
        compiler_params=_cparams("parallel", "parallel"),
        name="ada_mod",
    )(c_all, w_ada_bf, b_ada.reshape(depth * N_MOD, 1, d))
    return out.reshape(depth * N_MOD, r, 1, d)


def _conv_post(y, clg, clb, ogc):
    yn = _layer_norm(y, clg, clb)
    ys = yn * jax.nn.sigmoid(yn)
    return _rms_scale(ys, ogc).astype(BF16)


def _inproj_prompt_kernel(x_ref, sc_ref, sh_ref, w_ref, wt_ref, wdw_ref, bdw_ref, clg_ref, clb_ref, ogc_ref,
                          yc_ref, qt_ref, kt_ref, vt_ref, kb_ref, vbt_ref, km_ref, cn_ref, ubuf, ushift,
                          *, tm, dc, da):
    i = pl.program_id(1)

    @pl.when(i == 0)
    def _():
        ubuf[0:CONV_HALO, :] = jnp.zeros((CONV_HALO, dc), F32)

    x = x_ref[0]
    h = (x * (1.0 + sc_ref[0, 0]) + sh_ref[0, 0]).astype(BF16)

    def qkv_piece(t0):
        ts = slice(t0, t0 + MOBA_BLOCK)
        ht = h[ts]
        qt_ref[:, ts] = _dot_t(wt_ref[0, 0:da, :], ht) * (HEAD_DIM ** -0.5 * LOG2_E)
        kt_ref[:, ts] = _dot_t(wt_ref[0, da:2 * da, :], ht)
        zvt = _dot_t(wt_ref[0, 2 * da:3 * da, :], ht)
        vt_ref[:, ts] = zvt
        vbt_ref[:, ts] = zvt.astype(BF16)
        zk = _dot(ht, w_ref[0, :, 2 * dc + da:2 * dc + 2 * da])
        kb_ref[ts, :] = zk.astype(BF16)
        r = t0 // MOBA_BLOCK
        km_ref[0, r:r + 1, :] = jnp.mean(zk, axis=0, keepdims=True)

    n_chunks = tm // CONV_ROWS
    n_pieces = tm // MOBA_BLOCK
    zc = _dot(h, w_ref[0, :, 0:2 * dc])
    qkv_piece(0)
    ubuf[CONV_HALO:CONV_HALO + tm, :] = zc[:, :dc] * jax.nn.sigmoid(zc[:, dc:])

    first = CONV_HALO - (CONV_WIDTH - 1)
    span = tm + CONV_HALO - SUBLANES
    for r in range(1, SUBLANES):
        ushift[r - 1] = ubuf[r:r + span, :]
    for c in range(n_chunks):
        if c and c % (n_chunks // n_pieces) == 0:
            qkv_piece((c // (n_chunks // n_pieces)) * MOBA_BLOCK)
        groups = []
        for g in range(dc // LANES):
            ls = slice(g * LANES, (g + 1) * LANES)
            acc = jnp.broadcast_to(bdw_ref[0, :, ls], (CONV_ROWS, LANES))
            for j in range(CONV_WIDTH):
                a, r = divmod(first + j, SUBLANES)
                r0 = c * CONV_ROWS + a * SUBLANES
                src = ubuf[r0:r0 + CONV_ROWS, ls] if r == 0 else ushift[r - 1, r0:r0 + CONV_ROWS, ls]
                acc = acc + wdw_ref[0, j:j + 1, ls] * src
            groups.append(acc)
        y = jnp.concatenate(groups, axis=1)
        yc_ref[c * CONV_ROWS:(c + 1) * CONV_ROWS, :] = _conv_post(y, clg_ref[0], clb_ref[0], ogc_ref[0])

    tail = ubuf[tm:tm + CONV_HALO, :]
    cn_ref[0] = tail
    ubuf[0:CONV_HALO, :] = tail


def _inproj_prompt_kernel_kv(*refs, n_in, **kw):
    _inproj_prompt_kernel(*refs[:n_in], *refs[n_in + 2:], **kw)


def _inproj_prompt(l, depth, x3, mod, w_in_bf, w_qkv_t, w_dw, b_dw3, clg3, clb3, ogc3, tm, kv_all):
    b, s, d = x3.shape
    dc = w_dw.shape[-1]
    da = (w_in_bf.shape[-1] - 2 * dc) // 3
    r = mod.shape[1]
    nt = s // tm
    nbt = tm // MOBA_BLOCK
    n = b * s
    prow = r - SUBLANES
    kw = dict(tm=tm, dc=dc, da=da)

    def vec(a):
        return pl.BlockSpec((1, 1, a.shape[-1]), lambda bi, i: (l, 0, 0))

    def rows(width):
        return pl.BlockSpec((tm, width), lambda bi, i: (bi * nt + i, 0))

    cols = pl.BlockSpec((da, tm), lambda bi, i: (bi, i))
    cols_l = pl.BlockSpec((da, tm), lambda bi, i: (l * b + bi, i))

    in_specs = [pl.BlockSpec((1, tm, d), lambda bi, i: (bi, i, 0)),
                pl.BlockSpec((1, 1, 1, d), lambda bi, i: (l * N_MOD + SC1, prow + bi, 0, 0)),
                pl.BlockSpec((1, 1, 1, d), lambda bi, i: (l * N_MOD + SH1, prow + bi, 0, 0)),
                pl.BlockSpec((1, d, w_in_bf.shape[-1]), lambda bi, i: (l, 0, 0)),
                pl.BlockSpec((1, 3 * da, d), lambda bi, i: (l, 0, 0)),
                pl.BlockSpec((1, CONV_WIDTH, dc), lambda bi, i: (l, 0, 0)),
                vec(b_dw3), vec(clg3), vec(clb3), vec(ogc3)]
    args = [x3, mod, mod, w_in_bf, w_qkv_t, w_dw, b_dw3, clg3, clb3, ogc3]
    n_in = len(args)
    if kv_all is None:
        kern, aliases = functools.partial(_inproj_prompt_kernel, **kw), {}
    else:
        kern = functools.partial(_inproj_prompt_kernel_kv, n_in=n_in, **kw)
        in_specs += [pl.BlockSpec(memory_space=pl.ANY)] * 2
        args += list(kv_all)
        aliases = {n_in: 2, n_in + 1: 3}

    return pl.pallas_call(
        kern,
        grid=(b, nt),
        in_specs=in_specs,
        out_specs=[rows(dc), cols, cols_l, cols_l, rows(da), cols,
                   pl.BlockSpec((1, nbt, da), lambda bi, i: (bi * nt + i, 0, 0)),
                   pl.BlockSpec((1, CONV_HALO, dc), lambda bi, i: (bi, 0, 0))],
        out_shape=[jax.ShapeDtypeStruct((n, dc), BF16),
                   jax.ShapeDtypeStruct((b * da, s), F32),
                   jax.ShapeDtypeStruct((depth * b * da, s), F32),
                   jax.ShapeDtypeStruct((depth * b * da, s), F32),
                   jax.ShapeDtypeStruct((n, da), BF16),
                   jax.ShapeDtypeStruct((b * da, s), BF16),
                   jax.ShapeDtypeStruct((b * nt, nbt, da), F32),
                   jax.ShapeDtypeStruct((b, CONV_HALO, dc), F32)],
        scratch_shapes=[pltpu.VMEM((CONV_HALO + tm, dc), F32),
                        pltpu.VMEM((SUBLANES - 1, tm + CONV_HALO - SUBLANES, dc), F32)],
        input_output_aliases=aliases,
        compiler_params=_cparams("parallel", "arbitrary"),
        name=f"inproj_prompt_{l}",
    )(*args)


def _inproj_sample_kernel(x_ref, sc_ref, sh_ref, st_ref, w_ref, wdw_ref, bdw_ref, clg_ref, clb_ref, ogc_ref,
                          yc_ref, q_ref, k_ref, v_ref, cn_ref, uext, *, tb, ln, dc, da):
    d = x_ref.shape[-1]
    h = (x_ref[...] * (1.0 + sc_ref[0]) + sh_ref[0]).reshape(tb * ln, d).astype(BF16)

    zc = _dot(h, w_ref[0, :, 0:2 * dc])
    u = zc[:, :dc] * jax.nn.sigmoid(zc[:, dc:])
    uext[:, 0:CONV_HALO, :] = st_ref[...]
    uext[:, CONV_HALO:CONV_HALO + ln, :] = u.reshape(tb, ln, dc)

    first = CONV_HALO - (CONV_WIDTH - 1)
    acc = jnp.broadcast_to(bdw_ref[0], (tb, ln, dc))
    for j in range(CONV_WIDTH):
        acc = acc + wdw_ref[0, j:j + 1, :] * uext[:, first + j:first + j + ln, :]
    yc_ref[...] = _conv_post(acc.reshape(tb * ln, dc), clg_ref[0], clb_ref[0], ogc_ref[0])
    cn_ref[...] = uext[:, ln:ln + CONV_HALO, :]

    c0 = 2 * dc
    q_ref[...] = _dot(h, w_ref[0, :, c0:c0 + da]) * (HEAD_DIM ** -0.5)
    k_ref[...] = _dot(h, w_ref[0, :, c0 + da:c0 + 2 * da])
    v_ref[...] = _dot(h, w_ref[0, :, c0 + 2 * da:c0 + 3 * da])


def _inproj_sample(l, x3, mod, state_pad, w_in_bf, w_dw, b_dw3, clg3, clb3, ogc3, tb):
    bd, ln, d = x3.shape
    dc = w_dw.shape[-1]
    da = (w_in_bf.shape[-1] - 2 * dc) // 3
    n = bd * ln
    kern = functools.partial(_inproj_sample_kernel, tb=tb, ln=ln, dc=dc, da=da)

    def vec(a):
        return pl.BlockSpec((1, 1, a.shape[-1]), lambda i: (l, 0, 0))

    def rows(width):
        return pl.BlockSpec((tb * ln, width), lambda i: (i, 0))

    return pl.pallas_call(
        kern,
        grid=(bd // tb,),
        in_specs=[pl.BlockSpec((tb, ln, d), lambda i: (i, 0, 0)),
                  pl.BlockSpec((1, tb, 1, d), lambda i: (l * N_MOD + SC1, i, 0, 0)),
                  pl.BlockSpec((1, tb, 1, d), lambda i: (l * N_MOD + SH1, i, 0, 0)),
                  pl.BlockSpec((None, tb, CONV_HALO, dc), lambda i: (l, i, 0, 0)),
                  pl.BlockSpec((1, d, w_in_bf.shape[-1]), lambda i: (l, 0, 0)),
                  pl.BlockSpec((1, CONV_WIDTH, dc), lambda i: (l, 0, 0)),
                  vec(b_dw3), vec(clg3), vec(clb3), vec(ogc3)],
        out_specs=[rows(dc), rows(da), rows(da), rows(da),
                   pl.BlockSpec((tb, CONV_HALO, dc), lambda i: (i, 0, 0))],
        out_shape=[jax.ShapeDtypeStruct((n, dc), BF16),
                   jax.ShapeDtypeStruct((n, da), F32),
                   jax.ShapeDtypeStruct((n, da), F32),
                   jax.ShapeDtypeStruct((n, da), F32),
                   jax.ShapeDtypeStruct((bd, CONV_HALO, dc), F32)],
        scratch_shapes=[pltpu.VMEM((tb, CONV_HALO + ln, dc), F32)],
        compiler_params=_cparams("parallel"),
        name=f"inproj_sample_{l}",
    )(x3, mod, mod, state_pad, w_in_bf, w_dw, b_dw3, clg3, clb3, ogc3)


def _topk_mask(gate, valid, pos, k, axis):
    g = jnp.where(valid, gate, -jnp.inf)
    sel = jnp.zeros(gate.shape, jnp.bool_)
    for _ in range(k):
        mx = jnp.max(g, axis=axis, keepdims=True)
        idx = jnp.min(jnp.where(g == mx, pos, float(gate.shape[axis])), axis=axis, keepdims=True)
        pick = (pos == idx) & (mx > -jnp.inf)
        sel = sel | pick
        g = jnp.where(pick, -jnp.inf, g)
    return sel


def _attn_prompt_kernel(qt_ref, kb_ref, vbt_ref, km_ref, o_ref, kaug, vaug, qaug, s_own, s_a, s_b, m_sc, acc_sc,
                        *, nb, kvb):
    tq = MOBA_BLOCK
    kc = kvb * MOBA_BLOCK
    lane = lax.broadcasted_iota(jnp.int32, (tq, LANES), 1)
    head_lanes = (lane < HEAD_DIM, lane >= HEAD_DIM)
    pen_base = (HEAD_DIM, 0)

    nhs = kaug.shape[0]
    for n in range(nb):
        for hh in range(nhs):
            pp, hi = divmod(hh, 2)
            kblk = kb_ref[n * tq:(n + 1) * tq, pp * LANES:(pp + 1) * LANES]
            onehot = jnp.where(lane == pen_base[hi] + n, 1.0, 0.0).astype(BF16)
            kaug[hh, n * tq:(n + 1) * tq, :] = jnp.where(head_lanes[hi], kblk, onehot)

    for hh in range(nhs):
        vaug[hh, 0:HEAD_DIM, :] = vbt_ref[hh * HEAD_DIM:(hh + 1) * HEAD_DIM, :]
        vaug[hh, HEAD_DIM:, :] = jnp.ones((ONES_ROWS, vaug.shape[-1]), BF16)

    lane_nb = lax.broadcasted_iota(jnp.int32, (nb, LANES), 1)
    km2 = []
    for pp in range(nhs // 2):
        km = km_ref[0, :, pp * LANES:(pp + 1) * LANES]
        km2.append(jnp.concatenate([jnp.where(lane_nb < HEAD_DIM, km, 0.0),
                                    jnp.where(lane_nb >= HEAD_DIM, km, 0.0)], axis=0))

    def scores(j, c, buf, qa, causal):
        k0 = pl.multiple_of(c * kc, kc)
        for hh in range(nhs):
            for sb in range(kvb):
                ks = k0 + sb * tq
                buf[hh, sb * tq:(sb + 1) * tq, :] = _dot(kaug[hh, pl.ds(ks, tq), :], qa[hh])
        if causal:
            own = pl.multiple_of((j % kvb) * tq, tq)
            key_i = lax.broadcasted_iota(jnp.int32, (tq, tq), 0)
            qry_i = lax.broadcasted_iota(jnp.int32, (tq, tq), 1)
            for hh in range(nhs):
                buf[hh, pl.ds(own, tq), :] = jnp.where(key_i <= qry_i, buf[hh, pl.ds(own, tq), :], NEG_BIG)

    def softmax_pv(c, buf):
        k0 = pl.multiple_of(c * kc, kc)
        m_hd = []
        for hh in range(nhs):
            m_cur = jnp.max(buf[hh, 0:tq, :], axis=0, keepdims=True)
            for sb in range(1, kvb):
                m_cur = jnp.maximum(m_cur, jnp.max(buf[hh, sb * tq:(sb + 1) * tq, :], axis=0, keepdims=True))
            m_hd.append(jnp.maximum(m_sc[hh], m_cur))
        for hh in range(nhs):
            m_row = m_hd[hh][0:1]
            alpha = jnp.exp2(m_sc[hh] - m_hd[hh])
            p_all = jnp.concatenate([jnp.exp2((buf[hh, sb * tq:(sb + 1) * tq, :] - m_row).astype(BF16))
                                     for sb in range(kvb)], axis=0)
            pv = _dot(vaug[hh, :, pl.ds(k0, kc)], p_all)
            acc_sc[hh] = alpha[0:1] * acc_sc[hh] + pv
            m_sc[hh] = m_hd[hh]

    def open_block(j, slot):
        j0 = pl.multiple_of(j * tq, tq)
        blk_i = lax.broadcasted_iota(jnp.int32, (nb, tq), 0)
        blk_f = blk_i.astype(F32)
        zfill = jnp.zeros((HEAD_DIM - nb, tq), F32)
        for pp in range(nhs // 2):
            qt = qt_ref[pp * LANES:(pp + 1) * LANES, pl.ds(j0, tq)]
            gate2 = _dot_3pass(km2[pp], qt)
            for hi in range(2):
                sel = _topk_mask(gate2[hi * nb:(hi + 1) * nb], blk_i < j, blk_f, min(MOBA_TOPK, nb), 0)
                pen_t = jnp.where(sel | (blk_i == j), 0.0, NEG_BIG)
                if hi == 0:
                    rows_t = [qt[0:HEAD_DIM], pen_t, zfill]
                else:
                    rows_t = [pen_t, zfill, qt[HEAD_DIM:2 * HEAD_DIM]]
                qaug[slot, 2 * pp + hi] = jnp.concatenate(rows_t, axis=0).astype(BF16)
        scores(j, j // kvb, s_own.at[slot], qaug.at[slot], True)

    n_steps = nb // kvb

    def query_block(j, carry):
        slot = j % 2
        m_sc[...] = jnp.full(m_sc.shape, -jnp.inf, F32)
        acc_sc[...] = jnp.zeros(acc_sc.shape, F32)
        c_own = j // kvb

        def tile_buf(i):
            if i == 0:
                return s_own.at[slot]
            return s_b if i % 2 else s_a

        for i in range(n_steps):
            @pl.when(i <= c_own)
            def _():
                if i + 1 < n_steps:
                    scores(j, jnp.maximum(c_own - i - 1, 0), tile_buf(i + 1), qaug.at[slot], False)
                softmax_pv(c_own - i, tile_buf(i))
                if i == 0:
                    open_block(jnp.minimum(j + 1, nb - 1), 1 - slot)

        j0 = pl.multiple_of(j * tq, tq)
        o_t = jnp.concatenate([acc_sc[hh, 0:HEAD_DIM] / acc_sc[hh, HEAD_DIM:HEAD_DIM + 1] for hh in range(nhs)],
                              axis=0)
        o_ref[pl.ds(j0, tq), :] = o_t.T
        return carry

    open_block(0, 0)
    lax.fori_loop(0, nb, query_block, 0)


def _attn_prompt(l, qt2, kb2, vbt2, km3, b, s):
    n, da = kb2.shape
    nb = s // MOBA_BLOCK
    hp = da // LANES
    tq = MOBA_BLOCK
    kvb = max(c for c in range(1, KV_TILE_BLOCKS + 1) if nb % c == 0)
    km3 = km3.reshape(b, nb, da)
    npp = ATTN_HEAD_PAIRS if hp % ATTN_HEAD_PAIRS == 0 else 1
    wid, nhs, steps = npp * LANES, 2 * npp, hp // npp
    kern = functools.partial(_attn_prompt_kernel, nb=nb, kvb=kvb)
    return pl.pallas_call(
        kern,
        grid=(b, steps),
        in_specs=[pl.BlockSpec((wid, s), lambda bi, p: (bi * steps + p, 0)),
                  pl.BlockSpec((s, wid), lambda bi, p: (bi, p)),
                  pl.BlockSpec((wid, s), lambda bi, p: (bi * steps + p, 0)),
                  pl.BlockSpec((1, nb, wid), lambda bi, p: (bi, 0, p))],
        out_specs=pl.BlockSpec((s, wid), lambda bi, p: (bi, p)),
        out_shape=jax.ShapeDtypeStruct((n, da), F32),
        scratch_shapes=[pltpu.VMEM((nhs, s, LANES), BF16),
                        pltpu.VMEM((nhs, HEAD_DIM + ONES_ROWS, s), BF16),
                        pltpu.VMEM((2, nhs, LANES, tq), BF16),
                        pltpu.VMEM((2, nhs, kvb * tq, tq), F32),
                        pltpu.VMEM((nhs, kvb * tq, tq), F32),
                        pltpu.VMEM((nhs, kvb * tq, tq), F32),
                        pltpu.VMEM((nhs, SUBLANES, tq), F32),
                        pltpu.VMEM((nhs, HEAD_DIM + ONES_ROWS, tq), F32)],
        compiler_params=_cparams("parallel", "parallel"),
        name=f"attn_prompt_{l}",
    )(qt2, kb2, vbt2, km3)


def _attn_sample_kernel(pt_ref, q_ref, kn_ref, vn_ref, *refs, n_pages, nh, ln, spq):
    del pt_ref
    o_ref, kbuf, vbuf = refs[2 * spq * n_pages:]
    for sq in range(spq):
        k_pages = refs[sq * n_pages:(sq + 1) * n_pages]
        v_pages = refs[(spq + sq) * n_pages:(spq + sq + 1) * n_pages]
        o_ref[sq] = _attn_sample_one(q_ref[sq], kn_ref[sq], vn_ref[sq], k_pages, v_pages, kbuf.at[sq], vbuf.at[sq],
                                     nh=nh, ln=ln)


def _attn_sample_one(q, kn_new, vn_new, k_pages, v_pages, kbuf, vbuf, *, nh, ln):
    n_pages = len(k_pages)
    da = nh * HEAD_DIM
    rows = nh * ln
    ppb = MOBA_BLOCK // PAGE_SIZE
    nbp = n_pages // ppb

    qt = jnp.concatenate([q] * nh, axis=0)
    row = lax.broadcasted_iota(jnp.int32, (rows, da), 0)
    lane = lax.broadcasted_iota(jnp.int32, (rows, da), 1)
    own_head = (lane // HEAD_DIM) == (row // ln)
    qbd = jnp.where(own_head, qt, 0.0)
    qbd_bf = qbd.astype(BF16)

    km_lane = lax.broadcasted_iota(jnp.int32, (da, LANES), 1)
    km_t = jnp.zeros((da, LANES), F32)
    for n in range(nbp):
        ksum = jnp.zeros((da, PAGE_SIZE), F32)
        for pg in range(ppb):
            p_i = n * ppb + pg
            kp = k_pages[p_i][0]
            ksum = ksum + kp
            kbuf[:, p_i * PAGE_SIZE:(p_i + 1) * PAGE_SIZE] = kp.astype(BF16)
            vbuf[:, p_i * PAGE_SIZE:(p_i + 1) * PAGE_SIZE] = v_pages[p_i][0].astype(BF16)
        kmean = jnp.sum(ksum, axis=1, keepdims=True) * (1.0 / MOBA_BLOCK)
        km_t = jnp.where(km_lane == n, kmean, km_t)

    gate = jnp.dot(qbd, km_t, preferred_element_type=F32, precision=lax.Precision.HIGHEST)
    glane = lax.broadcasted_iota(jnp.int32, (rows, LANES), 1)
    sel = _topk_mask(gate, glane < nbp, glane.astype(F32), min(MOBA_TOPK, nbp), 1)
    bias = jnp.where(sel, 0.0, NEG_BIG)

    zrows = jnp.zeros((ln, da), F32)
    kn = jnp.concatenate([kn_new, zrows], axis=0).astype(BF16)
    vn = jnp.concatenate([vn_new, zrows], axis=0).astype(BF16)
    s_own = _dot_t(qbd_bf, kn)
    orow = lax.broadcasted_iota(jnp.int32, (rows, 2 * ln), 0)
    ocol = lax.broadcasted_iota(jnp.int32, (rows, 2 * ln), 1)
    s_own = jnp.where(ocol <= orow % ln, s_own, NEG_BIG)
    m = jnp.max(s_own, axis=-1, keepdims=True)

    s_all = _dot(qbd_bf, kbuf[...])
    s_past = []
    for n in range(nbp):
        s = s_all[:, n * MOBA_BLOCK:(n + 1) * MOBA_BLOCK] + bias[:, n:n + 1]
        s_past.append(s)
        m = jnp.maximum(m, jnp.max(s, axis=-1, keepdims=True))

    p_own = jnp.exp(s_own - m)
    lsum = jnp.sum(p_own, axis=-1, keepdims=True)
    p_past = []
    for n in range(nbp):
        p = jnp.exp(s_past[n] - m)
        lsum = lsum + jnp.sum(p, axis=-1, keepdims=True)
        p_past.append(p.astype(BF16))
    acc = _dot(p_own.astype(BF16), vn) + _dot_t(jnp.concatenate(p_past, axis=1), vbuf[...])

    o = jnp.where(own_head, acc / lsum, 0.0)
    out = o[0:ln]
    for hh in range(1, nh):
        out = out + o[hh * ln:(hh + 1) * ln]
    return out


def _attn_sample(l, q3, k3, v3, ck_t, cv_t, pt_flat, n_pool, n_pages):
    bd, ln, da = q3.shape
    nh = da // HEAD_DIM
    spq = SAMPLE_ATTN_SEQS if bd % SAMPLE_ATTN_SEQS == 0 else 1
    kern = functools.partial(_attn_sample_kernel, n_pages=n_pages, nh=nh, ln=ln, spq=spq)

    def page_spec(sq, p_i):
        return pl.BlockSpec((1, da, PAGE_SIZE),
                            lambda i, pt: (l * n_pool + pt[(i * spq + sq) * n_pages + p_i], 0, 0))

    pages = [page_spec(sq, p_i) for sq in range(spq) for p_i in range(n_pages)]
    tok = pl.BlockSpec((spq, ln, da), lambda i, pt: (i, 0, 0))
    grid_spec = pltpu.PrefetchScalarGridSpec(
        num_scalar_prefetch=1,
        grid=(bd // spq,),
        in_specs=[tok, tok, tok] + pages * 2,
        out_specs=tok,
        scratch_shapes=[pltpu.VMEM((spq, da, n_pages * PAGE_SIZE), BF16),
                        pltpu.VMEM((spq, da, n_pages * PAGE_SIZE), BF16)],
    )
    return pl.pallas_call(
        kern,
        grid_spec=grid_spec,
        out_shape=jax.ShapeDtypeStruct((bd, ln, da), F32),
        compiler_params=_cparams("arbitrary"),
        name=f"attn_sample_{l}",
    )(pt_flat, q3, k3, v3, *([ck_t] * (spq * n_pages)), *([cv_t] * (spq * n_pages)))


def _outmlp_kernel(x_ref, yc_ref, ya_ref, g1_ref, sc2_ref, sh2_ref, g2_ref, oga_ref, wout_ref,
                   ln1g_ref, ln1b_ref, w1_ref, b1_ref, w2_ref, b2_ref, ln2g_ref, ln2b_ref, o_ref,
                   *, alpha, ff_chunk):
    x = x_ref[...]
    blk = x.shape
    rows, d = blk[0] * blk[1], blk[2]
    dc = yc_ref.shape[-1]
    dff = w1_ref.shape[-1]

    yan = _rms_scale(ya_ref[...], oga_ref[0]).astype(BF16)
    mix = _dot(yc_ref[...], wout_ref[0, 0:dc, :]) + _dot(yan, wout_ref[0, dc:, :])
    x1 = _layer_norm(alpha * x + (1.0 + g1_ref[0]) * mix.reshape(blk), ln1g_ref[0], ln1b_ref[0])

    h2 = (x1 * (1.0 + sc2_ref[0]) + sh2_ref[0]).reshape(rows, d).astype(BF16)
    f = jnp.broadcast_to(b2_ref[0], (rows, d))
    for c in range(dff // ff_chunk):
        cs = slice(c * ff_chunk, (c + 1) * ff_chunk)
        hid = jnp.maximum(_dot(h2, w1_ref[0, :, cs]) + b1_ref[0, :, cs], 0.0)
        f = f + _dot((hid * hid).astype(BF16), w2_ref[0, cs, :])
    o_ref[...] = _layer_norm(alpha * x1 + (1.0 + g2_ref[0]) * f.reshape(blk), ln2g_ref[0], ln2b_ref[0])


def _outmlp(l, x3, yc2, ya2, mod, mod_row0, per_row_mod, oga3, w_out_bf, ln1g3, ln1b3, w1_bf, b13, w2_bf,
            b23, ln2g3, ln2b3, alpha, grp, tag):
    a, r, d = x3.shape
    dc, da = yc2.shape[-1], ya2.shape[-1]
    dff = w1_bf.shape[-1]
    if per_row_mod:
        blk = (grp, r, d)
        grid = (a // grp,)
        xmap = lambda i: (i, 0, 0)
        rmap = lambda i: (i, 0)
        brows = grp * r
        mblk = (1, grp, 1, d)

        def mod_spec(comp):
            return pl.BlockSpec(mblk, lambda i: (l * N_MOD + comp, i, 0, 0))
        sem = ("parallel",)
    else:
        nt = r // grp
        blk = (1, grp, d)
        grid = (a, nt)
        xmap = lambda bi, i: (bi, i, 0)
        rmap = lambda bi, i: (bi * nt + i, 0)
        brows = grp
        mblk = (1, 1, 1, d)

        def mod_spec(comp):
            return pl.BlockSpec(mblk, lambda bi, i: (l * N_MOD + comp, mod_row0 + bi, 0, 0))
        sem = ("parallel", "parallel")

    nargs = len(grid)

    def const(shape):
        zeros = (0,) * (len(shape) - 1)
        if nargs == 1:
            return pl.BlockSpec(shape, lambda i: (l,) + zeros, pipeline_mode=pl.Buffered(1))
        return pl.BlockSpec(shape, lambda bi, i: (l,) + zeros, pipeline_mode=pl.Buffered(1))

    kern = functools.partial(_outmlp_kernel, alpha=alpha, ff_chunk=min(dff, 1024))
    return pl.pallas_call(
        kern,
        grid=grid,
        in_specs=[pl.BlockSpec(blk, xmap),
                  pl.BlockSpec((brows, dc), rmap),
                  pl.BlockSpec((brows, da), rmap),
                  mod_spec(G1), mod_spec(SC2), mod_spec(SH2), mod_spec(G2),
                  const((1, 1, da)), const((1, dc + da, d)),
                  const((1, 1, d)), const((1, 1, d)),
                  const((1, d, dff)), const((1, 1, dff)), const((1, dff, d)), const((1, 1, d)),
                  const((1, 1, d)), const((1, 1, d))],
        out_specs=pl.BlockSpec(blk, xmap),
        out_shape=jax.ShapeDtypeStruct((a, r, d), F32),
        compiler_params=_cparams(*sem),
        name=f"outmlp_{tag}_{l}",
    )(x3, yc2, ya2, mod, mod, mod, mod, oga3, w_out_bf, ln1g3, ln1b3, w1_bf, b13, w2_bf, b23, ln2g3, ln2b3)


def kernel(x_prompt, x_sample, cache_k, cache_v, state_conv, page_table, c_prompt, c_sample, ln0_g, ln0_b, w_ada, b_ada, w_in, w_dw, b_dw, conv_ln_g, conv_ln_b, out_g_conv, out_g_attn, w_out, ln1_g, ln1_b, w1, b1, w2, b2, ln2_g, ln2_b):
    b, s, d = x_prompt.shape
    bd, ln, _ = x_sample.shape
    depth = w_in.shape[0]
    dc = w_dw.shape[-1]
    da = out_g_attn.shape[-1]
    nh = da // HEAD_DIM
    n_pool = cache_k.shape[1]
    n_pages = page_table.shape[1]
    alpha = (2 * depth) ** 0.25
    assert (n_pages * PAGE_SIZE) % MOBA_BLOCK == 0 and ln <= MOBA_BLOCK and ln == SUBLANES
    assert s % MOBA_BLOCK == 0 and s // MOBA_BLOCK <= HEAD_DIM and da % LANES == 0
    tm = min(PROMPT_ROWS, s)
    tb = min(SAMPLE_SEQS, bd)

    w_ada_bf, w_in_bf, w_out_bf = w_ada.astype(BF16), w_in.astype(BF16), w_out.astype(BF16)
    w1_bf, w2_bf = w1.astype(BF16), w2.astype(BF16)
    w_qkv_t = w_in[:, :, 2 * dc:].transpose(0, 2, 1).astype(BF16)

    def vec3(a):
        return a.reshape(depth, 1, a.shape[-1])

    b_dw3, clg3, clb3, ogc3, oga3 = vec3(b_dw), vec3(conv_ln_g), vec3(conv_ln_b), vec3(out_g_conv), vec3(out_g_attn)
    ln1g3, ln1b3, ln2g3, ln2b3, b13, b23 = vec3(ln1_g), vec3(ln1_b), vec3(ln2_g), vec3(ln2_b), vec3(b1), vec3(b2)

    n_rows = bd + SUBLANES
    assert b <= SUBLANES and bd % SUBLANES == 0
    c_all = jnp.concatenate([c_sample, c_prompt, jnp.zeros((SUBLANES - b, d), F32)], axis=0)
    mod = _ada(c_all, w_ada_bf, b_ada)
    mod_row0 = n_rows - SUBLANES

    xp = _input_ln(x_prompt.reshape(b * s, d), ln0_g, ln0_b, tm).reshape(b, s, d)
    xs = _input_ln(x_sample.reshape(bd * ln, d), ln0_g, ln0_b, min(bd * ln, 512)).reshape(bd, ln, d)

    state_pad = jnp.pad(state_conv, ((0, 0), (0, 0), (CONV_HALO - (CONV_WIDTH - 1), 0), (0, 0)))
    ck2 = cache_k.transpose(0, 1, 3, 4, 2).reshape(depth * n_pool, da, PAGE_SIZE)
    cv2 = cache_v.transpose(0, 1, 3, 4, 2).reshape(depth * n_pool, da, PAGE_SIZE)
    pt_flat = page_table.reshape(-1).astype(jnp.int32)

    cp, ksm, vsm, csm = [], [], [], []
    kt_all = vt_all = None
    for l in range(depth):
        yc, qt2, kt_all, vt_all, kb2, vbt2, km3, cn = _inproj_prompt(
            l, depth, xp, mod, w_in_bf, w_qkv_t, w_dw, b_dw3, clg3, clb3, ogc3, tm,
            None if l == 0 else (kt_all, vt_all))
        ya = _attn_prompt(l, qt2, kb2, vbt2, km3, b, s)
        xp = _outmlp(l, xp, yc, ya, mod, mod_row0, False, oga3, w_out_bf, ln1g3, ln1b3, w1_bf, b13, w2_bf, b23,
                     ln2g3, ln2b3, alpha, tm, "prompt")
        cp.append(cn[:, CONV_HALO - (CONV_WIDTH - 1):])

        ycs, qs, ks, vs, cns = _inproj_sample(l, xs, mod, state_pad, w_in_bf, w_dw, b_dw3, clg3, clb3, ogc3, tb)
        yas = _attn_sample(l, qs.reshape(bd, ln, da), ks.reshape(bd, ln, da), vs.reshape(bd, ln, da),
                           ck2, cv2, pt_flat, n_pool, n_pages)
        xs = _outmlp(l, xs, ycs, yas.reshape(bd * ln, da), mod, 0, True, oga3, w_out_bf, ln1g3, ln1b3, w1_bf,
                     b13, w2_bf, b23, ln2g3, ln2b3, alpha, tb, "sample")
        ksm.append(ks.reshape(bd, ln, nh, HEAD_DIM))
        vsm.append(vs.reshape(bd, ln, nh, HEAD_DIM))
        csm.append(cns[:, CONV_HALO - (CONV_WIDTH - 1):])

    k_prompt = kt_all.reshape(depth, b, nh, HEAD_DIM, s).transpose(0, 1, 4, 2, 3)
    v_prompt = vt_all.reshape(depth, b, nh, HEAD_DIM, s).transpose(0, 1, 4, 2, 3)
    return (xp, xs, k_prompt, v_prompt, jnp.stack(cp), jnp.stack(ksm), jnp.stack(vsm), jnp.stack(csm))
```

```python
import functools

import jax
import jax.numpy as jnp
from jax import lax
from jax.experimental import pallas as pl
from jax.experimental.pallas import tpu as pltpu

F32 = jnp.float32
BF16 = jnp.bfloat16

LN_EPS = 1e-5
HEAD_DIM = 64
MOBA_BLOCK = 256
MOBA_TOPK = 3
CONV_WIDTH = 31
PAGE_SIZE = 128
N_MOD = 6
SH1, SC1, G1, SH2, SC2, G2 = range(N_MOD)

SUBLANES = 8
LANES = 128
CONV_HALO = 32
CONV_ROWS = 64
NEG_BIG = -1e30
LOG2_E = 1.4426950408889634
VMEM_LIMIT = 56 * 1024 * 1024

PROMPT_ROWS = 512
SAMPLE_SEQS = 32
KV_TILE_BLOCKS = 2
SAMPLE_ATTN_SEQS = 2
ATTN_HEAD_PAIRS = 2
ONES_ROWS = 16


def _cparams(*sem):
    return pltpu.CompilerParams(dimension_semantics=sem, vmem_limit_bytes=VMEM_LIMIT)


def _layer_norm(x, g, b):
    mu = jnp.mean(x, axis=-1, keepdims=True)
    xc = x - mu
    var = jnp.mean(xc * xc, axis=-1, keepdims=True)
    return xc * lax.rsqrt(var + LN_EPS) * g + b


def _rms_scale(x, g):
    return x * lax.rsqrt(jnp.mean(x * x, axis=-1, keepdims=True) + LN_EPS) * g


def _dot(a, b):
    return jnp.dot(a, b, preferred_element_type=F32)


def _split_bf16(x):
    hi = x.astype(BF16)
    return hi, (x - hi.astype(F32)).astype(BF16)


def _dot_3pass(a, b):
    a_hi, a_lo = _split_bf16(a)
    b_hi, b_lo = _split_bf16(b)
    return _dot(a_hi, b_hi) + (_dot(a_hi, b_lo) + _dot(a_lo, b_hi))


def _dot_t(a, b, precision=None):
    return lax.dot_general(a, b, (((1,), (1,)), ((), ())), preferred_element_type=F32, precision=precision)


def _ln_kernel(x_ref, g_ref, b_ref, o_ref):
    o_ref[...] = _layer_norm(x_ref[...], g_ref[...], b_ref[...])


def _input_ln(x2, g, b, rows):
    n, d = x2.shape
    return pl.pallas_call(
        _ln_kernel,
        grid=(n // rows,),
        in_specs=[pl.BlockSpec((rows, d), lambda i: (i, 0)),
                  pl.BlockSpec((1, d), lambda i: (0, 0)),
                  pl.BlockSpec((1, d), lambda i: (0, 0))],
        out_specs=pl.BlockSpec((rows, d), lambda i: (i, 0)),
        out_shape=jax.ShapeDtypeStruct((n, d), F32),
        compiler_params=_cparams("parallel"),
        name="input_ln",
    )(x2, g.reshape(1, d), b.reshape(1, d))


def _ada_kernel(c_ref, w_ref, b_ref, o_ref):
    c = c_ref[...]
    h = (c * jax.nn.sigmoid(c)).astype(BF16)
    o_ref[0] = _dot(h, w_ref[0]) + b_ref[0]


def _ada(c_all, w_ada_bf, b_ada):
    depth, d, _ = w_ada_bf.shape
    r = c_all.shape[0]
    out = pl.pallas_call(
        _ada_kernel,
        grid=(depth, N_MOD),
        in_specs=[pl.BlockSpec((r, d), lambda l, j: (0, 0)),
                  pl.BlockSpec((1, d, d), lambda l, j: (l, 0, j)),
                  pl.BlockSpec((1, 1, d), lambda l, j: (l * N_MOD + j, 0, 0))],
        out_specs=pl.BlockSpec((1, r, d), lambda l, j: (l * N_MOD + j, 0, 0)),
        out_shape=jax.ShapeDtypeStruct((depth * N_MOD, r, d), F32),
        compiler_params=_cparams("parallel", "parallel"),
        name="ada_mod",
    )(c_all, w_ada_bf, b_ada.reshape(depth * N_MOD, 1, d))
    return out.reshape(depth * N_MOD, r, 1, d)


def _conv_post(y, clg, clb, ogc):
    yn = _layer_norm(y, clg, clb)
    ys = yn * jax.nn.sigmoid(yn)
    return _rms_scale(ys, ogc).astype(BF16)


def _inproj_prompt_kernel(x_ref, sc_ref, sh_ref, w_ref, wt_ref, wdw_ref, bdw_ref, clg_ref, clb_ref, ogc_ref,
                          yc_ref, qt_ref, kt_ref, vt_ref, kb_ref, vbt_ref, km_ref, cn_ref, ubuf, ushift,
                          *, tm, dc, da, input_ln=None):
    i = pl.program_id(1)

    @pl.when(i == 0)
    def _():
        ubuf[0:CONV_HALO, :] = jnp.zeros((CONV_HALO, dc), F32)

    x = x_ref[0]
    if input_ln is not None:
        g_ref, b_ref, xln_ref = input_ln
        x = _layer_norm(x, g_ref[...], b_ref[...])
        xln_ref[0] = x
    h = (x * (1.0 + sc_ref[0, 0]) + sh_ref[0, 0]).astype(BF16)

    def qkv_piece(t0):
        ts = slice(t0, t0 + MOBA_BLOCK)
        ht = h[ts]
        qt_ref[:, ts] = _dot_t(wt_ref[0, 0:da, :], ht) * (HEAD_DIM ** -0.5 * LOG2_E)
        kt_ref[:, ts] = _dot_t(wt_ref[0, da:2 * da, :], ht)
        zvt = _dot_t(wt_ref[0, 2 * da:3 * da, :], ht)
        vt_ref[:, ts] = zvt
        vbt_ref[:, ts] = zvt.astype(BF16)
        zk = _dot(ht, w_ref[0, :, 2 * dc + da:2 * dc + 2 * da])
        kb_ref[ts, :] = zk.astype(BF16)
        r = t0 // MOBA_BLOCK
        km_ref[0, r:r + 1, :] = jnp.mean(zk, axis=0, keepdims=True)

    n_chunks = tm // CONV_ROWS
    n_pieces = tm // MOBA_BLOCK
    zc = _dot(h, w_ref[0, :, 0:2 * dc])
    qkv_piece(0)
    ubuf[CONV_HALO:CONV_HALO + tm, :] = zc[:, :dc] * jax.nn.sigmoid(zc[:, dc:])

    first = CONV_HALO - (CONV_WIDTH - 1)
    span = tm + CONV_HALO - SUBLANES
    for r in range(1, SUBLANES):
        ushift[r - 1] = ubuf[r:r + span, :]
    for c in range(n_chunks):
        if c and c % (n_chunks // n_pieces) == 0:
            qkv_piece((c // (n_chunks // n_pieces)) * MOBA_BLOCK)
        groups = []
        for g in range(dc // LANES):
            ls = slice(g * LANES, (g + 1) * LANES)
            acc = jnp.broadcast_to(bdw_ref[0, :, ls], (CONV_ROWS, LANES))
            for j in range(CONV_WIDTH):
                a, r = divmod(first + j, SUBLANES)
                r0 = c * CONV_ROWS + a * SUBLANES
                src = ubuf[r0:r0 + CONV_ROWS, ls] if r == 0 else ushift[r - 1, r0:r0 + CONV_ROWS, ls]
                acc = acc + wdw_ref[0, j:j + 1, ls] * src
            groups.append(acc)
        y = jnp.concatenate(groups, axis=1)
        yc_ref[c * CONV_ROWS:(c + 1) * CONV_ROWS, :] = _conv_post(y, clg_ref[0], clb_ref[0], ogc_ref[0])

    tail = ubuf[tm:tm + CONV_HALO, :]
    cn_ref[0] = tail
    ubuf[0:CONV_HALO, :] = tail


def _inproj_prompt_kernel_kv(*refs, n_in, **kw):
    _inproj_prompt_kernel(*refs[:n_in], *refs[n_in + 2:], **kw)


N_INPROJ_OUT = 8


def _inproj_prompt_kernel_ln(*refs, n_in, **kw):
    g_ref, b_ref = refs[n_in:n_in + 2]
    outs = refs[n_in + 2:n_in + 2 + N_INPROJ_OUT]
    xln_ref = refs[n_in + 2 + N_INPROJ_OUT]
    _inproj_prompt_kernel(*refs[:n_in], *outs, *refs[n_in + 3 + N_INPROJ_OUT:], input_ln=(g_ref, b_ref, xln_ref), **kw)


def _inproj_prompt(l, depth, x3, mod, w_in_bf, w_qkv_t, w_dw, b_dw3, clg3, clb3, ogc3, tm, kv_all, ln0=None):
    b, s, d = x3.shape
    dc = w_dw.shape[-1]
    da = (w_in_bf.shape[-1] - 2 * dc) // 3
    r = mod.shape[1]
    nt = s // tm
    nbt = tm // MOBA_BLOCK
    n = b * s
    prow = r - SUBLANES
    kw = dict(tm=tm, dc=dc, da=da)

    def vec(a):
        return pl.BlockSpec((1, 1, a.shape[-1]), lambda bi, i: (l, 0, 0))

    def rows(width):
        return pl.BlockSpec((tm, width), lambda bi, i: (bi * nt + i, 0))

    cols = pl.BlockSpec((da, tm), lambda bi, i: (bi, i))
    cols_l = pl.BlockSpec((da, tm), lambda bi, i: (l * b + bi, i))

    in_specs = [pl.BlockSpec((1, tm, d), lambda bi, i: (bi, i, 0)),
                pl.BlockSpec((1, 1, 1, d), lambda bi, i: (l * N_MOD + SC1, prow + bi, 0, 0)),
                pl.BlockSpec((1, 1, 1, d), lambda bi, i: (l * N_MOD + SH1, prow + bi, 0, 0)),
                pl.BlockSpec((1, d, w_in_bf.shape[-1]), lambda bi, i: (l, 0, 0)),
                pl.BlockSpec((1, 3 * da, d), lambda bi, i: (l, 0, 0)),
                pl.BlockSpec((1, CONV_WIDTH, dc), lambda bi, i: (l, 0, 0)),
                vec(b_dw3), vec(clg3), vec(clb3), vec(ogc3)]
    args = [x3, mod, mod, w_in_bf, w_qkv_t, w_dw, b_dw3, clg3, clb3, ogc3]
    n_in = len(args)
    out_specs = [rows(dc), cols, cols_l, cols_l, rows(da), cols,
                 pl.BlockSpec((1, nbt, da), lambda bi, i: (bi * nt + i, 0, 0)),
                 pl.BlockSpec((1, CONV_HALO, dc), lambda bi, i: (bi, 0, 0))]
    out_shape = [jax.ShapeDtypeStruct((n, dc), BF16),
                 jax.ShapeDtypeStruct((b * da, s), F32),
                 jax.ShapeDtypeStruct((depth * b * da, s), F32),
                 jax.ShapeDtypeStruct((depth * b * da, s), F32),
                 jax.ShapeDtypeStruct((n, da), BF16),
                 jax.ShapeDtypeStruct((b * da, s), BF16),
                 jax.ShapeDtypeStruct((b * nt, nbt, da), F32),
                 jax.ShapeDtypeStruct((b, CONV_HALO, dc), F32)]
    assert len(out_shape) == N_INPROJ_OUT and (kv_all is None) == (ln0 is not None)
    if kv_all is None:
        kern, aliases = functools.partial(_inproj_prompt_kernel_ln, n_in=n_in, **kw), {}
        in_specs += [pl.BlockSpec((1, d), lambda bi, i: (0, 0))] * 2
        args += [ln0[0].reshape(1, d), ln0[1].reshape(1, d)]
        out_specs.append(pl.BlockSpec((1, tm, d), lambda bi, i: (bi, i, 0)))
        out_shape.append(jax.ShapeDtypeStruct((b, s, d), F32))
    else:
        kern = functools.partial(_inproj_prompt_kernel_kv, n_in=n_in, **kw)
        in_specs += [pl.BlockSpec(memory_space=pl.ANY)] * 2
        args += list(kv_all)
        aliases = {n_in: 2, n_in + 1: 3}

    return pl.pallas_call(
        kern,
        grid=(b, nt),
        in_specs=in_specs,
        out_specs=out_specs,
        out_shape=out_shape,
        scratch_shapes=[pltpu.VMEM((CONV_HALO + tm, dc), F32),
                        pltpu.VMEM((SUBLANES - 1, tm + CONV_HALO - SUBLANES, dc), F32)],
        input_output_aliases=aliases,
        compiler_params=_cparams("parallel", "arbitrary"),
        name=f"inproj_prompt_{l}",
    )(*args)


def _inproj_sample_kernel(x_ref, sc_ref, sh_ref, st_ref, w_ref, wdw_ref, bdw_ref, clg_ref, clb_ref, ogc_ref,
                          yc_ref, q_ref, k_ref, v_ref, cn_ref, uext, *, tb, ln, dc, da):
    d = x_ref.shape[-1]
    h = (x_ref[...] * (1.0 + sc_ref[0]) + sh_ref[0]).reshape(tb * ln, d).astype(BF16)

    zc = _dot(h, w_ref[0, :, 0:2 * dc])
    u = zc[:, :dc] * jax.nn.sigmoid(zc[:, dc:])
    uext[:, 0:CONV_HALO, :] = st_ref[...]
    uext[:, CONV_HALO:CONV_HALO + ln, :] = u.reshape(tb, ln, dc)

    first = CONV_HALO - (CONV_WIDTH - 1)
    acc = jnp.broadcast_to(bdw_ref[0], (tb, ln, dc))
    for j in range(CONV_WIDTH):
        acc = acc + wdw_ref[0, j:j + 1, :] * uext[:, first + j:first + j + ln, :]
    yc_ref[...] = _conv_post(acc.reshape(tb * ln, dc), clg_ref[0], clb_ref[0], ogc_ref[0])
    cn_ref[...] = uext[:, ln:ln + CONV_HALO, :]

    c0 = 2 * dc
    q_ref[...] = _dot(h, w_ref[0, :, c0:c0 + da]) * (HEAD_DIM ** -0.5)
    k_ref[...] = _dot(h, w_ref[0, :, c0 + da:c0 + 2 * da])
    v_ref[...] = _dot(h, w_ref[0, :, c0 + 2 * da:c0 + 3 * da])


def _inproj_sample(l, x3, mod, state_pad, w_in_bf, w_dw, b_dw3, clg3, clb3, ogc3, tb):
    bd, ln, d = x3.shape
    dc = w_dw.shape[-1]
    da = (w_in_bf.shape[-1] - 2 * dc) // 3
    n = bd * ln
    kern = functools.partial(_inproj_sample_kernel, tb=tb, ln=ln, dc=dc, da=da)

    def vec(a):
        return pl.BlockSpec((1, 1, a.shape[-1]), lambda i: (l, 0, 0))

    def rows(width):
        return pl.BlockSpec((tb * ln, width), lambda i: (i, 0))

    return pl.pallas_call(
        kern,
        grid=(bd // tb,),
        in_specs=[pl.BlockSpec((tb, ln, d), lambda i: (i, 0, 0)),
                  pl.BlockSpec((1, tb, 1, d), lambda i: (l * N_MOD + SC1, i, 0, 0)),
                  pl.BlockSpec((1, tb, 1, d), lambda i: (l * N_MOD + SH1, i, 0, 0)),
                  pl.BlockSpec((None, tb, CONV_HALO, dc), lambda i: (l, i, 0, 0)),
                  pl.BlockSpec((1, d, w_in_bf.shape[-1]), lambda i: (l, 0, 0)),
                  pl.BlockSpec((1, CONV_WIDTH, dc), lambda i: (l, 0, 0)),
                  vec(b_dw3), vec(clg3), vec(clb3), vec(ogc3)],
        out_specs=[rows(dc), rows(da), rows(da), rows(da),
                   pl.BlockSpec((tb, CONV_HALO, dc), lambda i: (i, 0, 0))],
        out_shape=[jax.ShapeDtypeStruct((n, dc), BF16),
                   jax.ShapeDtypeStruct((n, da), F32),
                   jax.ShapeDtypeStruct((n, da), F32),
                   jax.ShapeDtypeStruct((n, da), F32),
                   jax.ShapeDtypeStruct((bd, CONV_HALO, dc), F32)],
        scratch_shapes=[pltpu.VMEM((tb, CONV_HALO + ln, dc), F32)],
        compiler_params=_cparams("parallel"),
        name=f"inproj_sample_{l}",
    )(x3, mod, mod, state_pad, w_in_bf, w_dw, b_dw3, clg3, clb3, ogc3)


def _topk_mask(gate, valid, pos, k, axis):
    g = jnp.where(valid, gate, -jnp.inf)
    sel = jnp.zeros(gate.shape, jnp.bool_)
    for _ in range(k):
        mx = jnp.max(g, axis=axis, keepdims=True)
        idx = jnp.min(jnp.where(g == mx, pos, float(gate.shape[axis])), axis=axis, keepdims=True)
        pick = (pos == idx) & (mx > -jnp.inf)
        sel = sel | pick
        g = jnp.where(pick, -jnp.inf, g)
    return sel


def _attn_prompt_kernel(qt_ref, kb_ref, vbt_ref, km_ref, o_ref, kaug, vaug, qaug, s_own, s_a, s_b, m_sc, acc_sc,
                        *, nb, kvb):
    tq = MOBA_BLOCK
    kc = kvb * MOBA_BLOCK
    lane = lax.broadcasted_iota(jnp.int32, (tq, LANES), 1)
    head_lanes = (lane < HEAD_DIM, lane >= HEAD_DIM)
    pen_base = (HEAD_DIM, 0)

    nhs = kaug.shape[0]
    for n in range(nb):
        for hh in range(nhs):
            pp, hi = divmod(hh, 2)
            kblk = kb_ref[n * tq:(n + 1) * tq, pp * LANES:(pp + 1) * LANES]
            onehot = jnp.where(lane == pen_base[hi] + n, 1.0, 0.0).astype(BF16)
            kaug[hh, n * tq:(n + 1) * tq, :] = jnp.where(head_lanes[hi], kblk, onehot)

    for hh in range(nhs):
        vaug[hh, 0:HEAD_DIM, :] = vbt_ref[hh * HEAD_DIM:(hh + 1) * HEAD_DIM, :]
        vaug[hh, HEAD_DIM:, :] = jnp.ones((ONES_ROWS, vaug.shape[-1]), BF16)

    lane_nb = lax.broadcasted_iota(jnp.int32, (nb, LANES), 1)
    km2 = []
    for pp in range(nhs // 2):
        km = km_ref[0, :, pp * LANES:(pp + 1) * LANES]
        km2.append(jnp.concatenate([jnp.where(lane_nb < HEAD_DIM, km, 0.0),
                                    jnp.where(lane_nb >= HEAD_DIM, km, 0.0)], axis=0))

    def scores(j, c, buf, qa, causal):
        k0 = pl.multiple_of(c * kc, kc)
        for hh in range(nhs):
            for sb in range(kvb):
                ks = k0 + sb * tq
                buf[hh, sb * tq:(sb + 1) * tq, :] = _dot(kaug[hh, pl.ds(ks, tq), :], qa[hh])
        if causal:
            own = pl.multiple_of((j % kvb) * tq, tq)
            key_i = lax.broadcasted_iota(jnp.int32, (tq, tq), 0)
            qry_i = lax.broadcasted_iota(jnp.int32, (tq, tq), 1)
            for hh in range(nhs):
                buf[hh, pl.ds(own, tq), :] = jnp.where(key_i <= qry_i, buf[hh, pl.ds(own, tq), :], NEG_BIG)

    def softmax_pv(c, buf):
        k0 = pl.multiple_of(c * kc, kc)
        m_hd = []
        for hh in range(nhs):
            m_cur = jnp.max(buf[hh, 0:tq, :], axis=0, keepdims=True)
            for sb in range(1, kvb):
                m_cur = jnp.maximum(m_cur, jnp.max(buf[hh, sb * tq:(sb + 1) * tq, :], axis=0, keepdims=True))
            m_hd.append(jnp.maximum(m_sc[hh], m_cur))
        p_hd = []
        for hh in range(nhs):
            m_row = m_hd[hh][0:1]
            p_hd.append(jnp.concatenate([jnp.exp2((buf[hh, sb * tq:(sb + 1) * tq, :] - m_row).astype(BF16))
                                         for sb in range(kvb)], axis=0))
        for hh in range(nhs):
            alpha = jnp.exp2(m_sc[hh] - m_hd[hh])
            pv = _dot(vaug[hh, :, pl.ds(k0, kc)], p_hd[hh])
            acc_sc[hh] = alpha[0:1] * acc_sc[hh] + pv
            m_sc[hh] = m_hd[hh]

    def open_block(j, slot):
        j0 = pl.multiple_of(j * tq, tq)
        blk_i = lax.broadcasted_iota(jnp.int32, (nb, tq), 0)
        blk_f = blk_i.astype(F32)
        zfill = jnp.zeros((HEAD_DIM - nb, tq), F32)
        for pp in range(nhs // 2):
            qt = qt_ref[pp * LANES:(pp + 1) * LANES, pl.ds(j0, tq)]
            gate2 = _dot_3pass(km2[pp], qt)
            for hi in range(2):
                sel = _topk_mask(gate2[hi * nb:(hi + 1) * nb], blk_i < j, blk_f, min(MOBA_TOPK, nb), 0)
                pen_t = jnp.where(sel | (blk_i == j), 0.0, NEG_BIG)
                if hi == 0:
                    rows_t = [qt[0:HEAD_DIM], pen_t, zfill]
                else:
                    rows_t = [pen_t, zfill, qt[HEAD_DIM:2 * HEAD_DIM]]
                qaug[slot, 2 * pp + hi] = jnp.concatenate(rows_t, axis=0).astype(BF16)
        scores(j, j // kvb, s_own.at[slot], qaug.at[slot], True)

    n_steps = nb // kvb

    def query_block(j, carry):
        slot = j % 2
        m_sc[...] = jnp.full(m_sc.shape, -jnp.inf, F32)
        acc_sc[...] = jnp.zeros(acc_sc.shape, F32)
        c_own = j // kvb

        def tile_buf(i):
            if i == 0:
                return s_own.at[slot]
            return s_b if i % 2 else s_a

        for i in range(n_steps):
            @pl.when(i <= c_own)
            def _():
                if i + 1 < n_steps:
                    scores(j, jnp.maximum(c_own - i - 1, 0), tile_buf(i + 1), qaug.at[slot], False)
                softmax_pv(c_own - i, tile_buf(i))
                if i == 0:
                    open_block(jnp.minimum(j + 1, nb - 1), 1 - slot)

        j0 = pl.multiple_of(j * tq, tq)
        o_t = jnp.concatenate([acc_sc[hh, 0:HEAD_DIM] / acc_sc[hh, HEAD_DIM:HEAD_DIM + 1] for hh in range(nhs)],
                              axis=0)
        o_ref[pl.ds(j0, tq), :] = o_t.T
        return carry

    open_block(0, 0)
    lax.fori_loop(0, nb, query_block, 0)


def _attn_prompt(l, qt2, kb2, vbt2, km3, b, s):
    n, da = kb2.shape
    nb = s // MOBA_BLOCK
    hp = da // LANES
    tq = MOBA_BLOCK
    kvb = max(c for c in range(1, KV_TILE_BLOCKS + 1) if nb % c == 0)
    km3 = km3.reshape(b, nb, da)
    npp = ATTN_HEAD_PAIRS if hp % ATTN_HEAD_PAIRS == 0 else 1
    wid, nhs, steps = npp * LANES, 2 * npp, hp // npp
    kern = functools.partial(_attn_prompt_kernel, nb=nb, kvb=kvb)
    return pl.pallas_call(
        kern,
        grid=(b, steps),
        in_specs=[pl.BlockSpec((wid, s), lambda bi, p: (bi * steps + p, 0)),
                  pl.BlockSpec((s, wid), lambda bi, p: (bi, p)),
                  pl.BlockSpec((wid, s), lambda bi, p: (bi * steps + p, 0)),
                  pl.BlockSpec((1, nb, wid), lambda bi, p: (bi, 0, p))],
        out_specs=pl.BlockSpec((s, wid), lambda bi, p: (bi, p)),
        out_shape=jax.ShapeDtypeStruct((n, da), F32),
        scratch_shapes=[pltpu.VMEM((nhs, s, LANES), BF16),
                        pltpu.VMEM((nhs, HEAD_DIM + ONES_ROWS, s), BF16),
                        pltpu.VMEM((2, nhs, LANES, tq), BF16),
                        pltpu.VMEM((2, nhs, kvb * tq, tq), F32),
                        pltpu.VMEM((nhs, kvb * tq, tq), F32),
                        pltpu.VMEM((nhs, kvb * tq, tq), F32),
                        pltpu.VMEM((nhs, SUBLANES, tq), F32),
                        pltpu.VMEM((nhs, HEAD_DIM + ONES_ROWS, tq), F32)],
        compiler_params=_cparams("parallel", "parallel"),
        name=f"attn_prompt_{l}",
    )(qt2, kb2, vbt2, km3)


def _attn_sample_kernel(pt_ref, q_ref, kn_ref, vn_ref, *refs, n_pages, nh, ln, spq):
    del pt_ref
    o_ref, kbuf, vbuf = refs[2 * spq * n_pages:]
    for sq in range(spq):
        k_pages = refs[sq * n_pages:(sq + 1) * n_pages]
        v_pages = refs[(spq + sq) * n_pages:(spq + sq + 1) * n_pages]
        o_ref[sq] = _attn_sample_one(q_ref[sq], kn_ref[sq], vn_ref[sq], k_pages, v_pages, kbuf.at[sq], vbuf.at[sq],
                                     nh=nh, ln=ln)


def _attn_sample_one(q, kn_new, vn_new, k_pages, v_pages, kbuf, vbuf, *, nh, ln):
    n_pages = len(k_pages)
    da = nh * HEAD_DIM
    rows = nh * ln
    ppb = MOBA_BLOCK // PAGE_SIZE
    nbp = n_pages // ppb

    qt = jnp.concatenate([q] * nh, axis=0)
    row = lax.broadcasted_iota(jnp.int32, (rows, da), 0)
    lane = lax.broadcasted_iota(jnp.int32, (rows, da), 1)
    own_head = (lane // HEAD_DIM) == (row // ln)
    qbd = jnp.where(own_head, qt, 0.0)
    qbd_bf = qbd.astype(BF16)

    km_lane = lax.broadcasted_iota(jnp.int32, (da, LANES), 1)
    km_t = jnp.zeros((da, LANES), F32)
    for n in range(nbp):
        ksum = jnp.zeros((da, PAGE_SIZE), F32)
        for pg in range(ppb):
            p_i = n * ppb + pg
            kp = k_pages[p_i][0]
            ksum = ksum + kp
            kbuf[:, p_i * PAGE_SIZE:(p_i + 1) * PAGE_SIZE] = kp.astype(BF16)
            vbuf[:, p_i * PAGE_SIZE:(p_i + 1) * PAGE_SIZE] = v_pages[p_i][0].astype(BF16)
        kmean = jnp.sum(ksum, axis=1, keepdims=True) * (1.0 / MOBA_BLOCK)
        km_t = jnp.where(km_lane == n, kmean, km_t)

    gate = jnp.dot(qbd, km_t, preferred_element_type=F32, precision=lax.Precision.HIGHEST)
    glane = lax.broadcasted_iota(jnp.int32, (rows, LANES), 1)
    sel = _topk_mask(gate, glane < nbp, glane.astype(F32), min(MOBA_TOPK, nbp), 1)
    bias = jnp.where(sel, 0.0, NEG_BIG)

    zrows = jnp.zeros((ln, da), F32)
    kn = jnp.concatenate([kn_new, zrows], axis=0).astype(BF16)
    vn = jnp.concatenate([vn_new, zrows], axis=0).astype(BF16)
    s_own = _dot_t(qbd_bf, kn)
    orow = lax.broadcasted_iota(jnp.int32, (rows, 2 * ln), 0)
    ocol = lax.broadcasted_iota(jnp.int32, (rows, 2 * ln), 1)
    s_own = jnp.where(ocol <= orow % ln, s_own, NEG_BIG)
    m = jnp.max(s_own, axis=-1, keepdims=True)

    s_all = _dot(qbd_bf, kbuf[...])
    s_past = []
    for n in range(nbp):
        s = s_all[:, n * MOBA_BLOCK:(n + 1) * MOBA_BLOCK] + bias[:, n:n + 1]
        s_past.append(s)
        m = jnp.maximum(m, jnp.max(s, axis=-1, keepdims=True))

    p_own = jnp.exp(s_own - m)
    lsum = jnp.sum(p_own, axis=-1, keepdims=True)
    p_past = []
    for n in range(nbp):
        p = jnp.exp(s_past[n] - m)
        lsum = lsum + jnp.sum(p, axis=-1, keepdims=True)
        p_past.append(p.astype(BF16))
    acc = _dot(p_own.astype(BF16), vn) + _dot_t(jnp.concatenate(p_past, axis=1), vbuf[...])

    o = jnp.where(own_head, acc / lsum, 0.0)
    out = o[0:ln]
    for hh in range(1, nh):
        out = out + o[hh * ln:(hh + 1) * ln]
    return out


def _attn_sample(l, q3, k3, v3, ck_t, cv_t, pt_flat, n_pool, n_pages):
    bd, ln, da = q3.shape
    nh = da // HEAD_DIM
    spq = SAMPLE_ATTN_SEQS if bd % SAMPLE_ATTN_SEQS == 0 else 1
    kern = functools.partial(_attn_sample_kernel, n_pages=n_pages, nh=nh, ln=ln, spq=spq)

    def page_spec(sq, p_i):
        return pl.BlockSpec((1, da, PAGE_SIZE),
                            lambda i, pt: (l * n_pool + pt[(i * spq + sq) * n_pages + p_i], 0, 0))

    pages = [page_spec(sq, p_i) for sq in range(spq) for p_i in range(n_pages)]
    tok = pl.BlockSpec((spq, ln, da), lambda i, pt: (i, 0, 0))
    grid_spec = pltpu.PrefetchScalarGridSpec(
        num_scalar_prefetch=1,
        grid=(bd // spq,),
        in_specs=[tok, tok, tok] + pages * 2,
        out_specs=tok,
        scratch_shapes=[pltpu.VMEM((spq, da, n_pages * PAGE_SIZE), BF16),
                        pltpu.VMEM((spq, da, n_pages * PAGE_SIZE), BF16)],
    )
    return pl.pallas_call(
        kern,
        grid_spec=grid_spec,
        out_shape=jax.ShapeDtypeStruct((bd, ln, da), F32),
        compiler_params=_cparams("arbitrary"),
        name=f"attn_sample_{l}",
    )(pt_flat, q3, k3, v3, *([ck_t] * (spq * n_pages)), *([cv_t] * (spq * n_pages)))


def _outmlp_kernel(x_ref, yc_ref, ya_ref, g1_ref, sc2_ref, sh2_ref, g2_ref, oga_ref, wout_ref,
                   ln1g_ref, ln1b_ref, w1_ref, b1_ref, w2_ref, b2_ref, ln2g_ref, ln2b_ref, o_ref,
                   *, alpha, ff_chunk):
    x = x_ref[...]
    blk = x.shape
    rows, d = blk[0] * blk[1], blk[2]
    dc = yc_ref.shape[-1]
    dff = w1_ref.shape[-1]

    yan = _rms_scale(ya_ref[...], oga_ref[0]).astype(BF16)
    mix = _dot(yc_ref[...], wout_ref[0, 0:dc, :]) + _dot(yan, wout_ref[0, dc:, :])
    x1 = _layer_norm(alpha * x + (1.0 + g1_ref[0]) * mix.reshape(blk), ln1g_ref[0], ln1b_ref[0])

    h2 = (x1 * (1.0 + sc2_ref[0]) + sh2_ref[0]).reshape(rows, d).astype(BF16)
    f = jnp.broadcast_to(b2_ref[0], (rows, d))
    for c in range(dff // ff_chunk):
        cs = slice(c * ff_chunk, (c + 1) * ff_chunk)
        hid = jnp.maximum(_dot(h2, w1_ref[0, :, cs]) + b1_ref[0, :, cs], 0.0)
        f = f + _dot((hid * hid).astype(BF16), w2_ref[0, cs, :])
    o_ref[...] = _layer_norm(alpha * x1 + (1.0 + g2_ref[0]) * f.reshape(blk), ln2g_ref[0], ln2b_ref[0])


def _outmlp(l, x3, yc2, ya2, mod, mod_row0, per_row_mod, oga3, w_out_bf, ln1g3, ln1b3, w1_bf, b13, w2_bf,
            b23, ln2g3, ln2b3, alpha, grp, tag):
    a, r, d = x3.shape
    dc, da = yc2.shape[-1], ya2.shape[-1]
    dff = w1_bf.shape[-1]
    if per_row_mod:
        blk = (grp, r, d)
        grid = (a // grp,)
        xmap = lambda i: (i, 0, 0)
        rmap = lambda i: (i, 0)
        brows = grp * r
        mblk = (1, grp, 1, d)

        def mod_spec(comp):
            return pl.BlockSpec(mblk, lambda i: (l * N_MOD + comp, i, 0, 0))
        sem = ("parallel",)
    else:
        nt = r // grp
        blk = (1, grp, d)
        grid = (a, nt)
        xmap = lambda bi, i: (bi, i, 0)
        rmap = lambda bi, i: (bi * nt + i, 0)
        brows = grp
        mblk = (1, 1, 1, d)

        def mod_spec(comp):
            return pl.BlockSpec(mblk, lambda bi, i: (l * N_MOD + comp, mod_row0 + bi, 0, 0))
        sem = ("parallel", "parallel")

    nargs = len(grid)

    def const(shape):
        zeros = (0,) * (len(shape) - 1)
        if nargs == 1:
            return pl.BlockSpec(shape, lambda i: (l,) + zeros, pipeline_mode=pl.Buffered(1))
        return pl.BlockSpec(shape, lambda bi, i: (l,) + zeros, pipeline_mode=pl.Buffered(1))

    kern = functools.partial(_outmlp_kernel, alpha=alpha, ff_chunk=min(dff, 1024))
    return pl.pallas_call(
        kern,
        grid=grid,
        in_specs=[pl.BlockSpec(blk, xmap),
                  pl.BlockSpec((brows, dc), rmap),
                  pl.BlockSpec((brows, da), rmap),
                  mod_spec(G1), mod_spec(SC2), mod_spec(SH2), mod_spec(G2),
                  const((1, 1, da)), const((1, dc + da, d)),
                  const((1, 1, d)), const((1, 1, d)),
                  const((1, d, dff)), const((1, 1, dff)), const((1, dff, d)), const((1, 1, d)),
                  const((1, 1, d)), const((1, 1, d))],
        out_specs=pl.BlockSpec(blk, xmap),
        out_shape=jax.ShapeDtypeStruct((a, r, d), F32),
        compiler_params=_cparams(*sem),
        name=f"outmlp_{tag}_{l}",
    )(x3, yc2, ya2, mod, mod, mod, mod, oga3, w_out_bf, ln1g3, ln1b3, w1_bf, b13, w2_bf, b23, ln2g3, ln2b3)


def kernel(x_prompt, x_sample, cache_k, cache_v, state_conv, page_table, c_prompt, c_sample, ln0_g, ln0_b, w_ada, b_ada, w_in, w_dw, b_dw, conv_ln_g, conv_ln_b, out_g_conv, out_g_attn, w_out, ln1_g, ln1_b, w1, b1, w2, b2, ln2_g, ln2_b):
    b, s, d = x_prompt.shape
    bd, ln, _ = x_sample.shape
    depth = w_in.shape[0]
    dc = w_dw.shape[-1]
    da = out_g_attn.shape[-1]
    nh = da // HEAD_DIM
    n_pool = cache_k.shape[1]
    n_pages = page_table.shape[1]
    alpha = (2 * depth) ** 0.25
    assert (n_pages * PAGE_SIZE) % MOBA_BLOCK == 0 and ln <= MOBA_BLOCK and ln == SUBLANES
    assert s % MOBA_BLOCK == 0 and s // MOBA_BLOCK <= HEAD_DIM and da % LANES == 0
    tm = min(PROMPT_ROWS, s)
    tb = min(SAMPLE_SEQS, bd)

    w_ada_bf, w_in_bf, w_out_bf = w_ada.astype(BF16), w_in.astype(BF16), w_out.astype(BF16)
    w1_bf, w2_bf = w1.astype(BF16), w2.astype(BF16)
    w_qkv_t = w_in[:, :, 2 * dc:].transpose(0, 2, 1).astype(BF16)

    def vec3(a):
        return a.reshape(depth, 1, a.shape[-1])

    b_dw3, clg3, clb3, ogc3, oga3 = vec3(b_dw), vec3(conv_ln_g), vec3(conv_ln_b), vec3(out_g_conv), vec3(out_g_attn)
    ln1g3, ln1b3, ln2g3, ln2b3, b13, b23 = vec3(ln1_g), vec3(ln1_b), vec3(ln2_g), vec3(ln2_b), vec3(b1), vec3(b2)

    n_rows = bd + SUBLANES
    assert b <= SUBLANES and bd % SUBLANES == 0
    c_all = jnp.concatenate([c_sample, c_prompt, jnp.zeros((SUBLANES - b, d), F32)], axis=0)
    mod = _ada(c_all, w_ada_bf, b_ada)
    mod_row0 = n_rows - SUBLANES

    xp = x_prompt
    xs = _input_ln(x_sample.reshape(bd * ln, d), ln0_g, ln0_b, min(bd * ln, 512)).reshape(bd, ln, d)

    state_pad = jnp.pad(state_conv, ((0, 0), (0, 0), (CONV_HALO - (CONV_WIDTH - 1), 0), (0, 0)))
    ck2 = cache_k.transpose(0, 1, 3, 4, 2).reshape(depth * n_pool, da, PAGE_SIZE)
    cv2 = cache_v.transpose(0, 1, 3, 4, 2).reshape(depth * n_pool, da, PAGE_SIZE)
    pt_flat = page_table.reshape(-1).astype(jnp.int32)

    cp, ksm, vsm, csm = [], [], [], []
    kt_all = vt_all = None
    for l in range(depth):
        if l == 0:
            yc, qt2, kt_all, vt_all, kb2, vbt2, km3, cn, xp = _inproj_prompt(
                l, depth, xp, mod, w_in_bf, w_qkv_t, w_dw, b_dw3, clg3, clb3, ogc3, tm, None, (ln0_g, ln0_b))
        else:
            yc, qt2, kt_all, vt_all, kb2, vbt2, km3, cn = _inproj_prompt(
                l, depth, xp, mod, w_in_bf, w_qkv_t, w_dw, b_dw3, clg3, clb3, ogc3, tm, (kt_all, vt_all))
        ya = _attn_prompt(l, qt2, kb2, vbt2, km3, b, s)
        xp = _outmlp(l, xp, yc, ya, mod, mod_row0, False, oga3, w_out_bf, ln1g3, ln1b3, w1_bf, b13, w2_bf, b23,
                     ln2g3, ln2b3, alpha, tm, "prompt")
        cp.append(cn[:, CONV_HALO - (CONV_WIDTH - 1):])

        ycs, qs, ks, vs, cns = _inproj_sample(l, xs, mod, state_pad, w_in_bf, w_dw, b_dw3, clg3, clb3, ogc3, tb)
        yas = _attn_sample(l, qs.reshape(bd, ln, da), ks.reshape(bd, ln, da), vs.reshape(bd, ln, da),
                           ck2, cv2, pt_flat, n_pool, n_pages)
        xs = _outmlp(l, xs, ycs, yas.reshape(bd * ln, da), mod, 0, True, oga3, w_out_bf, ln1g3, ln1b3, w1_bf,
                     b13, w2_bf, b23, ln2g3, ln2b3, alpha, tb, "sample")
        ksm.append(ks.reshape(bd, ln, nh, HEAD_DIM))
        vsm.append(vs.reshape(bd, ln, nh, HEAD_DIM))
        csm.append(cns[:, CONV_HALO - (CONV_WIDTH - 1):])

    k_prompt = kt_all.reshape(depth, b, nh, HEAD_DIM, s).transpose(0, 1, 4, 2, 3)
    v_prompt = vt_all.reshape(depth, b, nh, HEAD_DIM, s).transpose(0, 1, 4, 2, 3)
    return (xp, xs, k_prompt, v_prompt, jnp.stack(cp), jnp.stack(ksm), jnp.stack(vsm), jnp.stack(csm))
```

```python
import functools

import jax
import jax.numpy as jnp
from jax import lax
from jax.experimental import pallas as pl
from jax.experimental.pallas import tpu as pltpu

F32 = jnp.float32
BF16 = jnp.bfloat16

LN_EPS = 1e-5
HEAD_DIM = 64
MOBA_BLOCK = 256
MOBA_TOPK = 3
CONV_WIDTH = 31
PAGE_SIZE = 128
N_MOD = 6
SH1, SC1, G1, SH2, SC2, G2 = range(N_MOD)

SUBLANES = 8
LANES = 128
CONV_HALO = 32
CONV_ROWS = 64
NEG_BIG = -1e30
LOG2_E = 1.4426950408889634
VMEM_LIMIT = 56 * 1024 * 1024

PROMPT_ROWS = 512
SAMPLE_SEQS = 32
KV_TILE_BLOCKS = 2
SAMPLE_ATTN_SEQS = 2
ATTN_HEAD_PAIRS = 2
ONES_ROWS = 16


def _cparams(*sem):
    return pltpu.CompilerParams(dimension_semantics=sem, vmem_limit_bytes=VMEM_LIMIT)


def _layer_norm(x, g, b):
    mu = jnp.mean(x, axis=-1, keepdims=True)
    xc = x - mu
    var = jnp.mean(xc * xc, axis=-1, keepdims=True)
    return xc * lax.rsqrt(var + LN_EPS) * g + b


def _rms_scale(x, g):
    return x * lax.rsqrt(jnp.mean(x * x, axis=-1, keepdims=True) + LN_EPS) * g


def _dot(a, b):
    return jnp.dot(a, b, preferred_element_type=F32)


def _split_bf16(x):
    hi = x.astype(BF16)
    return hi, (x - hi.astype(F32)).astype(BF16)


def _dot_3pass(a, b):
    a_hi, a_lo = _split_bf16(a)
    b_hi, b_lo = _split_bf16(b)
    return _dot(a_hi, b_hi) + (_dot(a_hi, b_lo) + _dot(a_lo, b_hi))


def _dot_t(a, b, precision=None):
    return lax.dot_general(a, b, (((1,), (1,)), ((), ())), preferred_element_type=F32, precision=precision)


def _ln_kernel(x_ref, g_ref, b_ref, o_ref):
    o_ref[...] = _layer_norm(x_ref[...], g_ref[...], b_ref[...])


def _input_ln(x2, g, b, rows):
    n, d = x2.shape
    return pl.pallas_call(
        _ln_kernel,
        grid=(n // rows,),
        in_specs=[pl.BlockSpec((rows, d), lambda i: (i, 0)),
                  pl.BlockSpec((1, d), lambda i: (0, 0)),
                  pl.BlockSpec((1, d), lambda i: (0, 0))],
        out_specs=pl.BlockSpec((rows, d), lambda i: (i, 0)),
        out_shape=jax.ShapeDtypeStruct((n, d), F32),
        compiler_params=_cparams("parallel"),
        name="input_ln",
    )(x2, g.reshape(1, d), b.reshape(1, d))


def _ada_kernel(c_ref, w_ref, b_ref, o_ref):
    c = c_ref[...]
    h = (c * jax.nn.sigmoid(c)).astype(BF16)
    o_ref[0] = _dot(h, w_ref[0]) + b_ref[0]


def _ada(c_all, w_ada_bf, b_ada):
    depth, d, _ = w_ada_bf.shape
    r = c_all.shape[0]
    out = pl.pallas_call(
        _ada_kernel,
        grid=(depth, N_MOD),
        in_specs=[pl.BlockSpec((r, d), lambda l, j: (0, 0)),
                  pl.BlockSpec((1, d, d), lambda l, j: (l, 0, j)),
                  pl.BlockSpec((1, 1, d), lambda l, j: (l * N_MOD + j, 0, 0))],
        out_specs=pl.BlockSpec((1, r, d), lambda l, j: (l * N_MOD + j, 0, 0)),
        out_shape=jax.ShapeDtypeStruct((depth * N_MOD, r, d), F32),
        compiler_params=_cparams("parallel", "parallel"),
        name="ada_mod",
    )(c_all, w_ada_bf, b_ada.reshape(depth * N_MOD, 1, d))
    return out.reshape(depth * N_MOD, r, 1, d)


def _conv_post(y, clg, clb, ogc):
    yn = _layer_norm(y, clg, clb)
    ys = yn * jax.nn.sigmoid(yn)
    return _rms_scale(ys, ogc).astype(BF16)


def _inproj_prompt_kernel(x_ref, sc_ref, sh_ref, w_ref, wt_ref, wdw_ref, bdw_ref, clg_ref, clb_ref, ogc_ref,
                          yc_ref, qt_ref, kt_ref, vt_ref, kb_ref, vbt_ref, km_ref, cn_ref, ubuf, ushift,
                          *, tm, dc, da, input_ln=None):
    i = pl.program_id(1)

    @pl.when(i == 0)
    def _():
        ubuf[0:CONV_HALO, :] = jnp.zeros((CONV_HALO, dc), F32)

    x = x_ref[0]
    if input_ln is not None:
        g_ref, b_ref, xln_ref = input_ln
        x = _layer_norm(x, g_ref[...], b_ref[...])
        xln_ref[0] = x
    if kt_ref.shape[0] > 1:
        kt_ref[1:] = jnp.zeros((kt_ref.shape[0] - 1,) + kt_ref.shape[1:], F32)
        vt_ref[1:] = jnp.zeros((vt_ref.shape[0] - 1,) + vt_ref.shape[1:], F32)
    h = (x * (1.0 + sc_ref[0, 0]) + sh_ref[0, 0]).astype(BF16)

    def qkv_piece(t0):
        ts = slice(t0, t0 + MOBA_BLOCK)
        ht = h[ts]
        qt_ref[:, ts] = _dot_t(wt_ref[0, 0:da, :], ht) * (HEAD_DIM ** -0.5 * LOG2_E)
        kt_ref[0, :, ts] = _dot_t(wt_ref[0, da:2 * da, :], ht)
        zvt = _dot_t(wt_ref[0, 2 * da:3 * da, :], ht)
        vt_ref[0, :, ts] = zvt
        vbt_ref[:, ts] = zvt.astype(BF16)
        zk = _dot(ht, w_ref[0, :, 2 * dc + da:2 * dc + 2 * da])
        kb_ref[ts, :] = zk.astype(BF16)
        r = t0 // MOBA_BLOCK
        km_ref[0, r:r + 1, :] = jnp.mean(zk, axis=0, keepdims=True)

    n_chunks = tm // CONV_ROWS
    n_pieces = tm // MOBA_BLOCK
    zc = _dot(h, w_ref[0, :, 0:2 * dc])
    qkv_piece(0)
    ubuf[CONV_HALO:CONV_HALO + tm, :] = zc[:, :dc] * jax.nn.sigmoid(zc[:, dc:])

    first = CONV_HALO - (CONV_WIDTH - 1)
    span = tm + CONV_HALO - SUBLANES
    for r in range(1, SUBLANES):
        ushift[r - 1] = ubuf[r:r + span, :]
    for c in range(n_chunks):
        if c and c % (n_chunks // n_pieces) == 0:
            qkv_piece((c // (n_chunks // n_pieces)) * MOBA_BLOCK)
        groups = []
        for g in range(dc // LANES):
            ls = slice(g * LANES, (g + 1) * LANES)
            acc = jnp.broadcast_to(bdw_ref[0, :, ls], (CONV_ROWS, LANES))
            for j in range(CONV_WIDTH):
                a, r = divmod(first + j, SUBLANES)
                r0 = c * CONV_ROWS + a * SUBLANES
                src = ubuf[r0:r0 + CONV_ROWS, ls] if r == 0 else ushift[r - 1, r0:r0 + CONV_ROWS, ls]
                acc = acc + wdw_ref[0, j:j + 1, ls] * src
            groups.append(acc)
        y = jnp.concatenate(groups, axis=1)
        yc_ref[c * CONV_ROWS:(c + 1) * CONV_ROWS, :] = _conv_post(y, clg_ref[0], clb_ref[0], ogc_ref[0])

    tail = ubuf[tm:tm + CONV_HALO, :]
    cn_ref[0] = tail
    ubuf[0:CONV_HALO, :] = tail


def _inproj_prompt_kernel_kv(*refs, n_in, **kw):
    _inproj_prompt_kernel(*refs[:n_in], *refs[n_in + 2:], **kw)


N_INPROJ_OUT = 8


def _inproj_prompt_kernel_ln(*refs, n_in, **kw):
    g_ref, b_ref = refs[n_in:n_in + 2]
    outs = refs[n_in + 2:n_in + 2 + N_INPROJ_OUT]
    xln_ref = refs[n_in + 2 + N_INPROJ_OUT]
    _inproj_prompt_kernel(*refs[:n_in], *outs, *refs[n_in + 3 + N_INPROJ_OUT:], input_ln=(g_ref, b_ref, xln_ref), **kw)


def _inproj_prompt(l, depth, x3, mod, w_in_bf, w_qkv_t, w_dw, b_dw3, clg3, clb3, ogc3, tm, kv_all, ln0=None):
    b, s, d = x3.shape
    dc = w_dw.shape[-1]
    da = (w_in_bf.shape[-1] - 2 * dc) // 3
    r = mod.shape[1]
    nt = s // tm
    nbt = tm // MOBA_BLOCK
    n = b * s
    prow = r - SUBLANES
    kw = dict(tm=tm, dc=dc, da=da)

    def vec(a):
        return pl.BlockSpec((1, 1, a.shape[-1]), lambda bi, i: (l, 0, 0))

    def rows(width):
        return pl.BlockSpec((tm, width), lambda bi, i: (bi * nt + i, 0))

    cols = pl.BlockSpec((da, tm), lambda bi, i: (bi, i))
    cols_l = pl.BlockSpec((depth if kv_all is None else 1, da, tm),
                          lambda bi, i: (0 if kv_all is None else l, bi, i))

    in_specs = [pl.BlockSpec((1, tm, d), lambda bi, i: (bi, i, 0)),
                pl.BlockSpec((1, 1, 1, d), lambda bi, i: (l * N_MOD + SC1, prow + bi, 0, 0)),
                pl.BlockSpec((1, 1, 1, d), lambda bi, i: (l * N_MOD + SH1, prow + bi, 0, 0)),
                pl.BlockSpec((1, d, w_in_bf.shape[-1]), lambda bi, i: (l, 0, 0)),
                pl.BlockSpec((1, 3 * da, d), lambda bi, i: (l, 0, 0)),
                pl.BlockSpec((1, CONV_WIDTH, dc), lambda bi, i: (l, 0, 0)),
                vec(b_dw3), vec(clg3), vec(clb3), vec(ogc3)]
    args = [x3, mod, mod, w_in_bf, w_qkv_t, w_dw, b_dw3, clg3, clb3, ogc3]
    n_in = len(args)
    out_specs = [rows(dc), cols, cols_l, cols_l, rows(da), cols,
                 pl.BlockSpec((1, nbt, da), lambda bi, i: (bi * nt + i, 0, 0)),
                 pl.BlockSpec((1, CONV_HALO, dc), lambda bi, i: (bi, 0, 0))]
    out_shape = [jax.ShapeDtypeStruct((n, dc), BF16),
                 jax.ShapeDtypeStruct((b * da, s), F32),
                 jax.ShapeDtypeStruct((depth, b * da, s), F32),
                 jax.ShapeDtypeStruct((depth, b * da, s), F32),
                 jax.ShapeDtypeStruct((n, da), BF16),
                 jax.ShapeDtypeStruct((b * da, s), BF16),
                 jax.ShapeDtypeStruct((b * nt, nbt, da), F32),
                 jax.ShapeDtypeStruct((b, CONV_HALO, dc), F32)]
    assert len(out_shape) == N_INPROJ_OUT and (kv_all is None) == (ln0 is not None)
    if kv_all is None:
        kern, aliases = functools.partial(_inproj_prompt_kernel_ln, n_in=n_in, **kw), {}
        in_specs += [pl.BlockSpec((1, d), lambda bi, i: (0, 0))] * 2
        args += [ln0[0].reshape(1, d), ln0[1].reshape(1, d)]
        out_specs.append(pl.BlockSpec((1, tm, d), lambda bi, i: (bi, i, 0)))
        out_shape.append(jax.ShapeDtypeStruct((b, s, d), F32))
    else:
        kern = functools.partial(_inproj_prompt_kernel_kv, n_in=n_in, **kw)
        in_specs += [pl.BlockSpec(memory_space=pl.ANY)] * 2
        args += list(kv_all)
        aliases = {n_in: 2, n_in + 1: 3}

    return pl.pallas_call(
        kern,
        grid=(b, nt),
        in_specs=in_specs,
        out_specs=out_specs,
        out_shape=out_shape,
        scratch_shapes=[pltpu.VMEM((CONV_HALO + tm, dc), F32),
                        pltpu.VMEM((SUBLANES - 1, tm + CONV_HALO - SUBLANES, dc), F32)],
        input_output_aliases=aliases,
        compiler_params=_cparams("parallel", "arbitrary"),
        name=f"inproj_prompt_{l}",
    )(*args)


def _inproj_sample_kernel(x_ref, sc_ref, sh_ref, st_ref, w_ref, wdw_ref, bdw_ref, clg_ref, clb_ref, ogc_ref,
                          yc_ref, q_ref, k_ref, v_ref, cn_ref, uext, *, tb, ln, dc, da):
    d = x_ref.shape[-1]
    h = (x_ref[...] * (1.0 + sc_ref[0]) + sh_ref[0]).reshape(tb * ln, d).astype(BF16)

    zc = _dot(h, w_ref[0, :, 0:2 * dc])
    u = zc[:, :dc] * jax.nn.sigmoid(zc[:, dc:])
    uext[:, 0:CONV_HALO, :] = st_ref[...]
    uext[:, CONV_HALO:CONV_HALO + ln, :] = u.reshape(tb, ln, dc)

    first = CONV_HALO - (CONV_WIDTH - 1)
    acc = jnp.broadcast_to(bdw_ref[0], (tb, ln, dc))
    for j in range(CONV_WIDTH):
        acc = acc + wdw_ref[0, j:j + 1, :] * uext[:, first + j:first + j + ln, :]
    yc_ref[...] = _conv_post(acc.reshape(tb * ln, dc), clg_ref[0], clb_ref[0], ogc_ref[0])
    cn_ref[...] = uext[:, ln:ln + CONV_HALO, :]

    c0 = 2 * dc
    q_ref[...] = _dot(h, w_ref[0, :, c0:c0 + da]) * (HEAD_DIM ** -0.5)
    k_ref[...] = _dot(h, w_ref[0, :, c0 + da:c0 + 2 * da])
    v_ref[...] = _dot(h, w_ref[0, :, c0 + 2 * da:c0 + 3 * da])


def _inproj_sample(l, x3, mod, state_pad, w_in_bf, w_dw, b_dw3, clg3, clb3, ogc3, tb):
    bd, ln, d = x3.shape
    dc = w_dw.shape[-1]
    da = (w_in_bf.shape[-1] - 2 * dc) // 3
    n = bd * ln
    kern = functools.partial(_inproj_sample_kernel, tb=tb, ln=ln, dc=dc, da=da)

    def vec(a):
        return pl.BlockSpec((1, 1, a.shape[-1]), lambda i: (l, 0, 0))

    def rows(width):
        return pl.BlockSpec((tb * ln, width), lambda i: (i, 0))

    return pl.pallas_call(
        kern,
        grid=(bd // tb,),
        in_specs=[pl.BlockSpec((tb, ln, d), lambda i: (i, 0, 0)),
                  pl.BlockSpec((1, tb, 1, d), lambda i: (l * N_MOD + SC1, i, 0, 0)),
                  pl.BlockSpec((1, tb, 1, d), lambda i: (l * N_MOD + SH1, i, 0, 0)),
                  pl.BlockSpec((None, tb, CONV_HALO, dc), lambda i: (l, i, 0, 0)),
                  pl.BlockSpec((1, d, w_in_bf.shape[-1]), lambda i: (l, 0, 0)),
                  pl.BlockSpec((1, CONV_WIDTH, dc), lambda i: (l, 0, 0)),
                  vec(b_dw3), vec(clg3), vec(clb3), vec(ogc3)],
        out_specs=[rows(dc), rows(da), rows(da), rows(da),
                   pl.BlockSpec((tb, CONV_HALO, dc), lambda i: (i, 0, 0))],
        out_shape=[jax.ShapeDtypeStruct((n, dc), BF16),
                   jax.ShapeDtypeStruct((n, da), F32),
                   jax.ShapeDtypeStruct((n, da), F32),
                   jax.ShapeDtypeStruct((n, da), F32),
                   jax.ShapeDtypeStruct((bd, CONV_HALO, dc), F32)],
        scratch_shapes=[pltpu.VMEM((tb, CONV_HALO + ln, dc), F32)],
        compiler_params=_cparams("parallel"),
        name=f"inproj_sample_{l}",
    )(x3, mod, mod, state_pad, w_in_bf, w_dw, b_dw3, clg3, clb3, ogc3)


def _topk_mask(gate, valid, pos, k, axis):
    g = jnp.where(valid, gate, -jnp.inf)
    sel = jnp.zeros(gate.shape, jnp.bool_)
    for _ in range(k):
        mx = jnp.max(g, axis=axis, keepdims=True)
        idx = jnp.min(jnp.where(g == mx, pos, float(gate.shape[axis])), axis=axis, keepdims=True)
        pick = (pos == idx) & (mx > -jnp.inf)
        sel = sel | pick
        g = jnp.where(pick, -jnp.inf, g)
    return sel


def _attn_prompt_kernel(qt_ref, kb_ref, vbt_ref, km_ref, o_ref, kaug, vaug, qaug, s_own, s_a, s_b, m_sc, acc_sc,
                        *, nb, kvb):
    tq = MOBA_BLOCK
    kc = kvb * MOBA_BLOCK
    lane = lax.broadcasted_iota(jnp.int32, (tq, LANES), 1)
    head_lanes = (lane < HEAD_DIM, lane >= HEAD_DIM)
    pen_base = (HEAD_DIM, 0)

    nhs = kaug.shape[0]
    for n in range(nb):
        for hh in range(nhs):
            pp, hi = divmod(hh, 2)
            kblk = kb_ref[n * tq:(n + 1) * tq, pp * LANES:(pp + 1) * LANES]
            onehot = jnp.where(lane == pen_base[hi] + n, 1.0, 0.0).astype(BF16)
            kaug[hh, n * tq:(n + 1) * tq, :] = jnp.where(head_lanes[hi], kblk, onehot)

    for hh in range(nhs):
        vaug[hh, 0:HEAD_DIM, :] = vbt_ref[hh * HEAD_DIM:(hh + 1) * HEAD_DIM, :]
        vaug[hh, HEAD_DIM:, :] = jnp.ones((ONES_ROWS, vaug.shape[-1]), BF16)

    lane_nb = lax.broadcasted_iota(jnp.int32, (nb, LANES), 1)
    km2 = []
    for pp in range(nhs // 2):
        km = km_ref[0, :, pp * LANES:(pp + 1) * LANES]
        km2.append(jnp.concatenate([jnp.where(lane_nb < HEAD_DIM, km, 0.0),
                                    jnp.where(lane_nb >= HEAD_DIM, km, 0.0)], axis=0))

    def scores(j, c, buf, qa, causal):
        k0 = pl.multiple_of(c * kc, kc)
        for hh in range(nhs):
            for sb in range(kvb):
                ks = k0 + sb * tq
                buf[hh, sb * tq:(sb + 1) * tq, :] = _dot(kaug[hh, pl.ds(ks, tq), :], qa[hh])
        if causal:
            own = pl.multiple_of((j % kvb) * tq, tq)
            key_i = lax.broadcasted_iota(jnp.int32, (tq, tq), 0)
            qry_i = lax.broadcasted_iota(jnp.int32, (tq, tq), 1)
            for hh in range(nhs):
                buf[hh, pl.ds(own, tq), :] = jnp.where(key_i <= qry_i, buf[hh, pl.ds(own, tq), :], NEG_BIG)

    def softmax_pv(c, buf):
        k0 = pl.multiple_of(c * kc, kc)
        m_hd = []
        for hh in range(nhs):
            m_cur = jnp.max(buf[hh, 0:tq, :], axis=0, keepdims=True)
            for sb in range(1, kvb):
                m_cur = jnp.maximum(m_cur, jnp.max(buf[hh, sb * tq:(sb + 1) * tq, :], axis=0, keepdims=True))
            m_hd.append(jnp.maximum(m_sc[hh], m_cur))
        p_hd = []
        for hh in range(nhs):
            m_row = m_hd[hh][0:1]
            p_hd.append(jnp.concatenate([jnp.exp2((buf[hh, sb * tq:(sb + 1) * tq, :] - m_row).astype(BF16))
                                         for sb in range(kvb)], axis=0))
        for hh in range(nhs):
            alpha = jnp.exp2(m_sc[hh] - m_hd[hh])
            pv = _dot(vaug[hh, :, pl.ds(k0, kc)], p_hd[hh])
            acc_sc[hh] = alpha[0:1] * acc_sc[hh] + pv
            m_sc[hh] = m_hd[hh]

    def open_block(j, slot):
        j0 = pl.multiple_of(j * tq, tq)
        blk_i = lax.broadcasted_iota(jnp.int32, (nb, tq), 0)
        blk_f = blk_i.astype(F32)
        zfill = jnp.zeros((HEAD_DIM - nb, tq), F32)
        for pp in range(nhs // 2):
            qt = qt_ref[pp * LANES:(pp + 1) * LANES, pl.ds(j0, tq)]
            gate2 = _dot_3pass(km2[pp], qt)
            for hi in range(2):
                sel = _topk_mask(gate2[hi * nb:(hi + 1) * nb], blk_i < j, blk_f, min(MOBA_TOPK, nb), 0)
                pen_t = jnp.where(sel | (blk_i == j), 0.0, NEG_BIG)
                if hi == 0:
                    rows_t = [qt[0:HEAD_DIM], pen_t, zfill]
                else:
                    rows_t = [pen_t, zfill, qt[HEAD_DIM:2 * HEAD_DIM]]
                qaug[slot, 2 * pp + hi] = jnp.concatenate(rows_t, axis=0).astype(BF16)
        scores(j, j // kvb, s_own.at[slot], qaug.at[slot], True)

    n_steps = nb // kvb

    def query_block(j, carry):
        slot = j % 2
        m_sc[...] = jnp.full(m_sc.shape, -jnp.inf, F32)
        acc_sc[...] = jnp.zeros(acc_sc.shape, F32)
        c_own = j // kvb

        def tile_buf(i):
            if i == 0:
                return s_own.at[slot]
            return s_b if i % 2 else s_a

        for i in range(n_steps):
            @pl.when(i <= c_own)
            def _():
                if i + 1 < n_steps:
                    scores(j, jnp.maximum(c_own - i - 1, 0), tile_buf(i + 1), qaug.at[slot], False)
                softmax_pv(c_own - i, tile_buf(i))
                if i == 0:
                    open_block(jnp.minimum(j + 1, nb - 1), 1 - slot)

        j0 = pl.multiple_of(j * tq, tq)
        o_t = jnp.concatenate([acc_sc[hh, 0:HEAD_DIM] / acc_sc[hh, HEAD_DIM:HEAD_DIM + 1] for hh in range(nhs)],
                              axis=0)
        o_ref[pl.ds(j0, tq), :] = o_t.T
        return carry

    open_block(0, 0)
    lax.fori_loop(0, nb, query_block, 0)


def _attn_prompt(l, qt2, kb2, vbt2, km3, b, s):
    n, da = kb2.shape
    nb = s // MOBA_BLOCK
    hp = da // LANES
    tq = MOBA_BLOCK
    kvb = max(c for c in range(1, KV_TILE_BLOCKS + 1) if nb % c == 0)
    km3 = km3.reshape(b, nb, da)
    npp = ATTN_HEAD_PAIRS if hp % ATTN_HEAD_PAIRS == 0 else 1
    wid, nhs, steps = npp * LANES, 2 * npp, hp // npp
    kern = functools.partial(_attn_prompt_kernel, nb=nb, kvb=kvb)
    return pl.pallas_call(
        kern,
        grid=(b, steps),
        in_specs=[pl.BlockSpec((wid, s), lambda bi, p: (bi * steps + p, 0)),
                  pl.BlockSpec((s, wid), lambda bi, p: (bi, p)),
                  pl.BlockSpec((wid, s), lambda bi, p: (bi * steps + p, 0)),
                  pl.BlockSpec((1, nb, wid), lambda bi, p: (bi, 0, p))],
        out_specs=pl.BlockSpec((s, wid), lambda bi, p: (bi, p)),
        out_shape=jax.ShapeDtypeStruct((n, da), F32),
        scratch_shapes=[pltpu.VMEM((nhs, s, LANES), BF16),
                        pltpu.VMEM((nhs, HEAD_DIM + ONES_ROWS, s), BF16),
                        pltpu.VMEM((2, nhs, LANES, tq), BF16),
                        pltpu.VMEM((2, nhs, kvb * tq, tq), F32),
                        pltpu.VMEM((nhs, kvb * tq, tq), F32),
                        pltpu.VMEM((nhs, kvb * tq, tq), F32),
                        pltpu.VMEM((nhs, SUBLANES, tq), F32),
                        pltpu.VMEM((nhs, HEAD_DIM + ONES_ROWS, tq), F32)],
        compiler_params=_cparams("parallel", "parallel"),
        name=f"attn_prompt_{l}",
    )(qt2, kb2, vbt2, km3)


def _attn_sample_kernel(pt_ref, q_ref, kn_ref, vn_ref, *refs, n_pages, nh, ln, spq):
    del pt_ref
    o_ref, kbuf, vbuf = refs[2 * spq * n_pages:]
    for sq in range(spq):
        k_pages = refs[sq * n_pages:(sq + 1) * n_pages]
        v_pages = refs[(spq + sq) * n_pages:(spq + sq + 1) * n_pages]
        o_ref[sq] = _attn_sample_one(q_ref[sq], kn_ref[sq], vn_ref[sq], k_pages, v_pages, kbuf.at[sq], vbuf.at[sq],
                                     nh=nh, ln=ln)


def _attn_sample_one(q, kn_new, vn_new, k_pages, v_pages, kbuf, vbuf, *, nh, ln):
    n_pages = len(k_pages)
    da = nh * HEAD_DIM
    rows = nh * ln
    ppb = MOBA_BLOCK // PAGE_SIZE
    nbp = n_pages // ppb

    qt = jnp.concatenate([q] * nh, axis=0)
    row = lax.broadcasted_iota(jnp.int32, (rows, da), 0)
    lane = lax.broadcasted_iota(jnp.int32, (rows, da), 1)
    own_head = (lane // HEAD_DIM) == (row // ln)
    qbd = jnp.where(own_head, qt, 0.0)
    qbd_bf = qbd.astype(BF16)

    km_lane = lax.broadcasted_iota(jnp.int32, (da, LANES), 1)
    km_t = jnp.zeros((da, LANES), F32)
    for n in range(nbp):
        ksum = jnp.zeros((da, PAGE_SIZE), F32)
        for pg in range(ppb):
            p_i = n * ppb + pg
            kp = k_pages[p_i][0]
            ksum = ksum + kp
            kbuf[:, p_i * PAGE_SIZE:(p_i + 1) * PAGE_SIZE] = kp.astype(BF16)
            vbuf[:, p_i * PAGE_SIZE:(p_i + 1) * PAGE_SIZE] = v_pages[p_i][0].astype(BF16)
        kmean = jnp.sum(ksum, axis=1, keepdims=True) * (1.0 / MOBA_BLOCK)
        km_t = jnp.where(km_lane == n, kmean, km_t)

    gate = jnp.dot(qbd, km_t, preferred_element_type=F32, precision=lax.Precision.HIGHEST)
    glane = lax.broadcasted_iota(jnp.int32, (rows, LANES), 1)
    sel = _topk_mask(gate, glane < nbp, glane.astype(F32), min(MOBA_TOPK, nbp), 1)
    bias = jnp.where(sel, 0.0, NEG_BIG)

    zrows = jnp.zeros((ln, da), F32)
    kn = jnp.concatenate([kn_new, zrows], axis=0).astype(BF16)
    vn = jnp.concatenate([vn_new, zrows], axis=0).astype(BF16)
    s_own = _dot_t(qbd_bf, kn)
    orow = lax.broadcasted_iota(jnp.int32, (rows, 2 * ln), 0)
    ocol = lax.broadcasted_iota(jnp.int32, (rows, 2 * ln), 1)
    s_own = jnp.where(ocol <= orow % ln, s_own, NEG_BIG)
    m = jnp.max(s_own, axis=-1, keepdims=True)

    s_all = _dot(qbd_bf, kbuf[...])
    s_past = []
    for n in range(nbp):
        s = s_all[:, n * MOBA_BLOCK:(n + 1) * MOBA_BLOCK] + bias[:, n:n + 1]
        s_past.append(s)
        m = jnp.maximum(m, jnp.max(s, axis=-1, keepdims=True))

    p_own = jnp.exp(s_own - m)
    lsum = jnp.sum(p_own, axis=-1, keepdims=True)
    p_past = []
    for n in range(nbp):
        p = jnp.exp(s_past[n] - m)
        lsum = lsum + jnp.sum(p, axis=-1, keepdims=True)
        p_past.append(p.astype(BF16))
    acc = _dot(p_own.astype(BF16), vn) + _dot_t(jnp.concatenate(p_past, axis=1), vbuf[...])

    o = jnp.where(own_head, acc / lsum, 0.0)
    out = o[0:ln]
    for hh in range(1, nh):
        out = out + o[hh * ln:(hh + 1) * ln]
    return out


def _attn_sample(l, q3, k3, v3, ck_t, cv_t, pt_flat, n_pool, n_pages):
    bd, ln, da = q3.shape
    nh = da // HEAD_DIM
    spq = SAMPLE_ATTN_SEQS if bd % SAMPLE_ATTN_SEQS == 0 else 1
    kern = functools.partial(_attn_sample_kernel, n_pages=n_pages, nh=nh, ln=ln, spq=spq)

    def page_spec(sq, p_i):
        return pl.BlockSpec((1, da, PAGE_SIZE),
                            lambda i, pt: (l * n_pool + pt[(i * spq + sq) * n_pages + p_i], 0, 0))

    pages = [page_spec(sq, p_i) for sq in range(spq) for p_i in range(n_pages)]
    tok = pl.BlockSpec((spq, ln, da), lambda i, pt: (i, 0, 0))
    grid_spec = pltpu.PrefetchScalarGridSpec(
        num_scalar_prefetch=1,
        grid=(bd // spq,),
        in_specs=[tok, tok, tok] + pages * 2,
        out_specs=tok,
        scratch_shapes=[pltpu.VMEM((spq, da, n_pages * PAGE_SIZE), BF16),
                        pltpu.VMEM((spq, da, n_pages * PAGE_SIZE), BF16)],
    )
    return pl.pallas_call(
        kern,
        grid_spec=grid_spec,
        out_shape=jax.ShapeDtypeStruct((bd, ln, da), F32),
        compiler_params=_cparams("arbitrary"),
        name=f"attn_sample_{l}",
    )(pt_flat, q3, k3, v3, *([ck_t] * (spq * n_pages)), *([cv_t] * (spq * n_pages)))


def _outmlp_kernel(x_ref, yc_ref, ya_ref, g1_ref, sc2_ref, sh2_ref, g2_ref, oga_ref, wout_ref,
                   ln1g_ref, ln1b_ref, w1_ref, b1_ref, w2_ref, b2_ref, ln2g_ref, ln2b_ref, o_ref,
                   *, alpha, ff_chunk):
    x = x_ref[...]
    blk = x.shape
    rows, d = blk[0] * blk[1], blk[2]
    dc = yc_ref.shape[-1]
    dff = w1_ref.shape[-1]

    yan = _rms_scale(ya_ref[...], oga_ref[0]).astype(BF16)
    mix = _dot(yc_ref[...], wout_ref[0, 0:dc, :]) + _dot(yan, wout_ref[0, dc:, :])
    x1 = _layer_norm(alpha * x + (1.0 + g1_ref[0]) * mix.reshape(blk), ln1g_ref[0], ln1b_ref[0])

    h2 = (x1 * (1.0 + sc2_ref[0]) + sh2_ref[0]).reshape(rows, d).astype(BF16)
    f = jnp.broadcast_to(b2_ref[0], (rows, d))
    for c in range(dff // ff_chunk):
        cs = slice(c * ff_chunk, (c + 1) * ff_chunk)
        hid = jnp.maximum(_dot(h2, w1_ref[0, :, cs]) + b1_ref[0, :, cs], 0.0)
        f = f + _dot((hid * hid).astype(BF16), w2_ref[0, cs, :])
    o_ref[...] = _layer_norm(alpha * x1 + (1.0 + g2_ref[0]) * f.reshape(blk), ln2g_ref[0], ln2b_ref[0])


def _outmlp(l, x3, yc2, ya2, mod, mod_row0, per_row_mod, oga3, w_out_bf, ln1g3, ln1b3, w1_bf, b13, w2_bf,
            b23, ln2g3, ln2b3, alpha, grp, tag):
    a, r, d = x3.shape
    dc, da = yc2.shape[-1], ya2.shape[-1]
    dff = w1_bf.shape[-1]
    if per_row_mod:
        blk = (grp, r, d)
        grid = (a // grp,)
        xmap = lambda i: (i, 0, 0)
        rmap = lambda i: (i, 0)
        brows = grp * r
        mblk = (1, grp, 1, d)

        def mod_spec(comp):
            return pl.BlockSpec(mblk, lambda i: (l * N_MOD + comp, i, 0, 0))
        sem = ("parallel",)
    else:
        nt = r // grp
        blk = (1, grp, d)
        grid = (a, nt)
        xmap = lambda bi, i: (bi, i, 0)
        rmap = lambda bi, i: (bi * nt + i, 0)
        brows = grp
        mblk = (1, 1, 1, d)

        def mod_spec(comp):
            return pl.BlockSpec(mblk, lambda bi, i: (l * N_MOD + comp, mod_row0 + bi, 0, 0))
        sem = ("parallel", "parallel")

    nargs = len(grid)

    def const(shape):
        zeros = (0,) * (len(shape) - 1)
        if nargs == 1:
            return pl.BlockSpec(shape, lambda i: (l,) + zeros, pipeline_mode=pl.Buffered(1))
        return pl.BlockSpec(shape, lambda bi, i: (l,) + zeros, pipeline_mode=pl.Buffered(1))

    kern = functools.partial(_outmlp_kernel, alpha=alpha, ff_chunk=min(dff, 1024))
    return pl.pallas_call(
        kern,
        grid=grid,
        in_specs=[pl.BlockSpec(blk, xmap),
                  pl.BlockSpec((brows, dc), rmap),
                  pl.BlockSpec((brows, da), rmap),
                  mod_spec(G1), mod_spec(SC2), mod_spec(SH2), mod_spec(G2),
                  const((1, 1, da)), const((1, dc + da, d)),
                  const((1, 1, d)), const((1, 1, d)),
                  const((1, d, dff)), const((1, 1, dff)), const((1, dff, d)), const((1, 1, d)),
                  const((1, 1, d)), const((1, 1, d))],
        out_specs=pl.BlockSpec(blk, xmap),
        out_shape=jax.ShapeDtypeStruct((a, r, d), F32),
        compiler_params=_cparams(*sem),
        name=f"outmlp_{tag}_{l}",
    )(x3, yc2, ya2, mod, mod, mod, mod, oga3, w_out_bf, ln1g3, ln1b3, w1_bf, b13, w2_bf, b23, ln2g3, ln2b3)


def kernel(x_prompt, x_sample, cache_k, cache_v, state_conv, page_table, c_prompt, c_sample, ln0_g, ln0_b, w_ada, b_ada, w_in, w_dw, b_dw, conv_ln_g, conv_ln_b, out_g_conv, out_g_attn, w_out, ln1_g, ln1_b, w1, b1, w2, b2, ln2_g, ln2_b):
    b, s, d = x_prompt.shape
    bd, ln, _ = x_sample.shape
    depth = w_in.shape[0]
    dc = w_dw.shape[-1]
    da = out_g_attn.shape[-1]
    nh = da // HEAD_DIM
    n_pool = cache_k.shape[1]
    n_pages = page_table.shape[1]
    alpha = (2 * depth) ** 0.25
    assert (n_pages * PAGE_SIZE) % MOBA_BLOCK == 0 and ln <= MOBA_BLOCK and ln == SUBLANES
    assert s % MOBA_BLOCK == 0 and s // MOBA_BLOCK <= HEAD_DIM and da % LANES == 0
    tm = min(PROMPT_ROWS, s)
    tb = min(SAMPLE_SEQS, bd)

    w_ada_bf, w_in_bf, w_out_bf = w_ada.astype(BF16), w_in.astype(BF16), w_out.astype(BF16)
    w1_bf, w2_bf = w1.astype(BF16), w2.astype(BF16)
    w_qkv_t = w_in[:, :, 2 * dc:].transpose(0, 2, 1).astype(BF16)

    def vec3(a):
        return a.reshape(depth, 1, a.shape[-1])

    b_dw3, clg3, clb3, ogc3, oga3 = vec3(b_dw), vec3(conv_ln_g), vec3(conv_ln_b), vec3(out_g_conv), vec3(out_g_attn)
    ln1g3, ln1b3, ln2g3, ln2b3, b13, b23 = vec3(ln1_g), vec3(ln1_b), vec3(ln2_g), vec3(ln2_b), vec3(b1), vec3(b2)

    n_rows = bd + SUBLANES
    assert b <= SUBLANES and bd % SUBLANES == 0
    c_all = jnp.concatenate([c_sample, c_prompt, jnp.zeros((SUBLANES - b, d), F32)], axis=0)
    mod = _ada(c_all, w_ada_bf, b_ada)
    mod_row0 = n_rows - SUBLANES

    xp = x_prompt
    xs = _input_ln(x_sample.reshape(bd * ln, d), ln0_g, ln0_b, min(bd * ln, 512)).reshape(bd, ln, d)

    state_pad = jnp.pad(state_conv, ((0, 0), (0, 0), (CONV_HALO - (CONV_WIDTH - 1), 0), (0, 0)))
    ck2 = cache_k.transpose(0, 1, 3, 4, 2).reshape(depth * n_pool, da, PAGE_SIZE)
    cv2 = cache_v.transpose(0, 1, 3, 4, 2).reshape(depth * n_pool, da, PAGE_SIZE)
    pt_flat = page_table.reshape(-1).astype(jnp.int32)

    cp, ksm, vsm, csm = [], [], [], []
    kt_all = vt_all = None
    for l in range(depth):
        if l == 0:
            yc, qt2, kt_all, vt_all, kb2, vbt2, km3, cn, xp = _inproj_prompt(
                l, depth, xp, mod, w_in_bf, w_qkv_t, w_dw, b_dw3, clg3, clb3, ogc3, tm, None, (ln0_g, ln0_b))
        else:
            yc, qt2, kt_all, vt_all, kb2, vbt2, km3, cn = _inproj_prompt(
                l, depth, xp, mod, w_in_bf, w_qkv_t, w_dw, b_dw3, clg3, clb3, ogc3, tm, (kt_all, vt_all))
        ya = _attn_prompt(l, qt2, kb2, vbt2, km3, b, s)
        xp = _outmlp(l, xp, yc, ya, mod, mod_row0, False, oga3, w_out_bf, ln1g3, ln1b3, w1_bf, b13, w2_bf, b23,
                     ln2g3, ln2b3, alpha, tm, "prompt")
        cp.append(cn[:, CONV_HALO - (CONV_WIDTH - 1):])

        ycs, qs, ks, vs, cns = _inproj_sample(l, xs, mod, state_pad, w_in_bf, w_dw, b_dw3, clg3, clb3, ogc3, tb)
        yas = _attn_sample(l, qs.reshape(bd, ln, da), ks.reshape(bd, ln, da), vs.reshape(bd, ln, da),
                           ck2, cv2, pt_flat, n_pool, n_pages)
        xs = _outmlp(l, xs, ycs, yas.reshape(bd * ln, da), mod, 0, True, oga3, w_out_bf, ln1g3, ln1b3, w1_bf,
                     b13, w2_bf, b23, ln2g3, ln2b3, alpha, tb, "sample")
        ksm.append(ks.reshape(bd, ln, nh, HEAD_DIM))
        vsm.append(vs.reshape(bd, ln, nh, HEAD_DIM))
        csm.append(cns[:, CONV_HALO - (CONV_WIDTH - 1):])

    k_prompt = kt_all.reshape(depth, b, nh, HEAD_DIM, s).transpose(0, 1, 4, 2, 3)
    v_prompt = vt_all.reshape(depth, b, nh, HEAD_DIM, s).transpose(0, 1, 4, 2, 3)
    return (xp, xs, k_prompt, v_prompt, jnp.stack(cp), jnp.stack(ksm), jnp.stack(vsm), jnp.stack(csm))
```
